```python
import math
import jax, jax.numpy as jnp
from jax import lax
import numpy as np

D_MODEL = 1024
BATCH = 8
SEQ = 4096
DEPTH = 4

N_MIXERS = 4
FFN_DIM = ((8 * D_MODEL // 3 + 127) // 128) * 128
DN_ALPHA = (2.0 * DEPTH) ** 0.25
DN_BETA = (8.0 * DEPTH) ** -0.25
ROPE_THETA = 500000.0
LN_EPS = 1e-5
RMS_EPS = 1e-6

CONF_KERNEL = 31
SCONV_KERNEL = 3

ATT_HEADS = D_MODEL // 64
ATT_HEAD_DIM = 64
ROPE_DIM = ATT_HEAD_DIM // 4
ATT_NOPE_DIM = ATT_HEAD_DIM - ROPE_DIM
Q_RANK = D_MODEL // 4
KV_RANK = D_MODEL // 8
IDX_HEADS = 8
IDX_DIM = ATT_HEAD_DIM
TOPK_MAX = 256
Q_BLOCK = 128

GDN_K_HEADS = D_MODEL // 128
GDN_V_HEADS = 2 * GDN_K_HEADS
GDN_K_DIM = 128
GDN_V_DIM = 128
GDN_CONV = 4
GDN_CHUNK = 64
GDN_QKV = 2 * GDN_K_HEADS * GDN_K_DIM + GDN_V_HEADS * GDN_V_DIM
GDN_IN = GDN_QKV + GDN_V_HEADS * GDN_V_DIM + 2 * GDN_V_HEADS

kernel_name = 'hybrid_conv_shortconv_dsa_gdn_trunk'


def _mixer_count(m):
    return (DEPTH + N_MIXERS - 1 - m) // N_MIXERS


def layer_norm(x, g, b):
    xf = x.astype(jnp.float32)
    mu = jnp.mean(xf, -1, keepdims=True)
    var = jnp.mean(jnp.square(xf - mu), -1, keepdims=True)
    return ((xf - mu) * lax.rsqrt(var + LN_EPS)).astype(x.dtype) * g + b


def rms_norm(x, g):
    xf = x.astype(jnp.float32)
    return (xf * lax.rsqrt(jnp.mean(xf * xf, -1, keepdims=True) + RMS_EPS)).astype(x.dtype) * g


def causal_dwconv(x, w):
    width, ch = w.shape
    return lax.conv_general_dilated(
        x, w[:, None, :].astype(x.dtype), window_strides=(1,), padding=[(width - 1, 0)],
        dimension_numbers=('NWC', 'WIO', 'NWC'), feature_group_count=ch)


def rope_tables(positions):
    inv = ROPE_THETA ** (-jnp.arange(0, ROPE_DIM, 2, dtype=jnp.float32) / ROPE_DIM)
    ang = positions.astype(jnp.float32)[..., None] * inv
    return jnp.cos(ang), jnp.sin(ang)


def apply_partial_rope(x, cos, sin):
    half = cos.shape[-1]
    x1, x2, rest = x[..., :half], x[..., half:2 * half], x[..., 2 * half:]
    c, s = cos.astype(x.dtype), sin.astype(x.dtype)
    return jnp.concatenate([x1 * c - x2 * s, x2 * c + x1 * s, rest], axis=-1)


def swiglu(x, w_gate, w_up, w_down):
    return (jax.nn.silu(x @ w_gate) * (x @ w_up)) @ w_down


def conformer_conv(h, w_in, w_dw, ln_g, ln_b, w_out):
    u = h @ w_in
    u = u[..., :D_MODEL] * jax.nn.sigmoid(u[..., D_MODEL:])
    u = causal_dwconv(u, w_dw)
    u = jax.nn.silu(layer_norm(u, ln_g, ln_b))
    return u @ w_out


def short_gated_conv(h, w_in, w_conv, w_out):
    bch = h @ w_in
    gate_b = bch[..., :D_MODEL]
    gate_c = bch[..., D_MODEL:2 * D_MODEL]
    val = bch[..., 2 * D_MODEL:]
    y = gate_b * causal_dwconv(gate_c * val, w_conv)
    return y @ w_out


def dsa_attention(h, cos, sin, w_dq, q_norm, w_uq, w_dkv, kv_norm, w_kr, w_uk, w_uv, w_o,
                  w_iq, w_ik, ik_g, ik_b, w_iw):
    bn, s_len, _ = h.shape
    n_blk = s_len // Q_BLOCK
    topk = min(TOPK_MAX, s_len // 4)
    cos_h, sin_h = cos[:, :, None, :], sin[:, :, None, :]
    c_q = rms_norm(h @ w_dq, q_norm)
    q = (c_q @ w_uq).reshape(bn, s_len, ATT_HEADS, ATT_HEAD_DIM)
    q_rope = apply_partial_rope(q[..., :ROPE_DIM], cos_h, sin_h)
    q_lat = jnp.einsum('bshn,hnc->bshc', q[..., ROPE_DIM:], w_uk)
    c_kv = rms_norm(h @ w_dkv, kv_norm)
    k_rope = apply_partial_rope(h @ w_kr, cos, sin)
    qi = apply_partial_rope((c_q @ w_iq).reshape(bn, s_len, IDX_HEADS, IDX_DIM), cos_h, sin_h)
    ki = apply_partial_rope(layer_norm(h @ w_ik, ik_g, ik_b), cos, sin)
    qi, ki = qi.astype(jnp.float32), ki.astype(jnp.float32)
    wi = (h @ w_iw).astype(jnp.float32) * (IDX_HEADS * IDX_DIM) ** -0.5

    def blocks(a):
        return jnp.swapaxes(a.reshape(bn, n_blk, Q_BLOCK, *a.shape[2:]), 0, 1)

    t_idx = jnp.arange(s_len, dtype=jnp.int32).reshape(n_blk, Q_BLOCK)
    key_idx = jnp.arange(s_len, dtype=jnp.int32)
    b_idx = jnp.arange(bn)[:, None, None]
    scale = ATT_HEAD_DIM ** -0.5

    def attend_block(args):
        ql, qr, qib, wib, t = args
        rel = jax.nn.relu(jnp.einsum('bqhd,bsd->bqhs', qib, ki))
        score = jnp.einsum('bqhs,bqh->bqs', rel, wib)
        score = jnp.where(key_idx[None, None, :] <= t[None, :, None], score, -jnp.inf)
        _, sel = lax.top_k(score, topk)
        valid = sel <= t[None, :, None]
        kv_sel = c_kv[b_idx, sel]
        kr_sel = k_rope[b_idx, sel]
        logits = (jnp.einsum('bqhc,bqkc->bqhk', ql, kv_sel)
                  + jnp.einsum('bqhr,bqkr->bqhk', qr, kr_sel))
        logits = jnp.where(valid[:, :, None, :], logits.astype(jnp.float32) * scale, -jnp.inf)
        p = jax.nn.softmax(logits, axis=-1).astype(kv_sel.dtype)
        return jnp.einsum('bqhk,bqkc->bqhc', p, kv_sel)

    o_lat = lax.map(attend_block, (blocks(q_lat), blocks(q_rope), blocks(qi), blocks(wi), t_idx))
    o_lat = jnp.swapaxes(o_lat, 0, 1).reshape(bn, s_len, ATT_HEADS, KV_RANK)
    o = jnp.einsum('bshc,hcd->bshd', o_lat, w_uv).reshape(bn, s_len, ATT_HEADS * ATT_HEAD_DIM)
    return o @ w_o


def gated_delta_rule(q, k, v, g, beta):
    bn, s_len, nh, dk = q.shape
    dv = v.shape[-1]
    n_ch, c = s_len // GDN_CHUNK, GDN_CHUNK
    q = q * lax.rsqrt(jnp.sum(q * q, -1, keepdims=True) + RMS_EPS) * dk ** -0.5
    k = k * lax.rsqrt(jnp.sum(k * k, -1, keepdims=True) + RMS_EPS)

    def chunks(a):
        a = a.reshape(bn, n_ch, c, nh, *a.shape[3:])
        return jnp.moveaxis(a, (1, 3), (0, 2))

    q, k, v, g, beta = chunks(q), chunks(k), chunks(v), chunks(g), chunks(beta)
    g = jnp.cumsum(g, axis=-1)
    ii = jnp.arange(c)
    tril = ii[:, None] >= ii[None, :]
    strict = ii[:, None] > ii[None, :]
    decay = jnp.exp(jnp.where(tril, g[..., :, None] - g[..., None, :], -jnp.inf))
    kb = k * beta[..., None]
    a_mat = jnp.where(strict, jnp.einsum('nbhid,nbhjd->nbhij', kb, k) * decay, 0.0)
    eye = jnp.eye(c, dtype=jnp.float32)
    t_mat = lax.linalg.triangular_solve(a_mat + eye, jnp.broadcast_to(eye, a_mat.shape),
                                        left_side=True, lower=True, unit_diagonal=True)
    u = jnp.einsum('nbhij,nbhjd->nbhid', t_mat, v * beta[..., None])
    w = jnp.einsum('nbhij,nbhjd->nbhid', t_mat, kb * jnp.exp(g)[..., None])
    intra = jnp.where(tril, jnp.einsum('nbhid,nbhjd->nbhij', q, k) * decay, 0.0)
    g_last = g[..., -1]
    k_tail = k * jnp.exp(g_last[..., None] - g)[..., None]
    q_head = q * jnp.exp(g)[..., None]

    def step(state, xs):
        q_n, kt_n, u_n, w_n, intra_n, gl_n = xs
        v_new = u_n - jnp.einsum('bhck,bhkv->bhcv', w_n, state)
        out = (jnp.einsum('bhck,bhkv->bhcv', q_n, state)
               + jnp.einsum('bhij,bhjv->bhiv', intra_n, v_new))
        state = state * jnp.exp(gl_n)[..., None, None] + jnp.einsum('bhck,bhcv->bhkv', kt_n, v_new)
        return state, out

    state0 = jnp.zeros((bn, nh, dk, dv), jnp.float32)
    _, out = lax.scan(step, state0, (q_head, k_tail, u, w, intra, g_last))
    return jnp.moveaxis(out, (0, 2), (1, 3)).reshape(bn, s_len, nh, dv)


def gated_deltanet(h, w_in, w_conv, a_log, dt_bias, norm_g, w_out):
    bn, s_len, _ = h.shape
    nk = GDN_K_HEADS * GDN_K_DIM
    nv = GDN_V_HEADS * GDN_V_DIM
    proj = h @ w_in
    qkv = jax.nn.silu(causal_dwconv(proj[..., :GDN_QKV], w_conv))
    rep = GDN_V_HEADS // GDN_K_HEADS
    q = jnp.repeat(qkv[..., :nk].reshape(bn, s_len, GDN_K_HEADS, GDN_K_DIM), rep, axis=2)
    k = jnp.repeat(qkv[..., nk:2 * nk].reshape(bn, s_len, GDN_K_HEADS, GDN_K_DIM), rep, axis=2)
    v = qkv[..., 2 * nk:].reshape(bn, s_len, GDN_V_HEADS, GDN_V_DIM)
    z = proj[..., GDN_QKV:GDN_QKV + nv].reshape(bn, s_len, GDN_V_HEADS, GDN_V_DIM)
    b_logit = proj[..., GDN_QKV + nv:GDN_QKV + nv + GDN_V_HEADS].astype(jnp.float32)
    a_in = proj[..., GDN_QKV + nv + GDN_V_HEADS:].astype(jnp.float32)
    beta = jax.nn.sigmoid(b_logit)
    g = -jnp.exp(a_log.astype(jnp.float32)) * jax.nn.softplus(a_in + dt_bias.astype(jnp.float32))
    o = gated_delta_rule(q.astype(jnp.float32), k.astype(jnp.float32), v.astype(jnp.float32),
                         g, beta).astype(h.dtype)
    o = rms_norm(o, norm_g) * jax.nn.silu(z)
    return o.reshape(bn, s_len, nv) @ w_out


def setup_inputs(seed: int = 0) -> dict:
    key = jax.random.key(seed)
    ks = iter(jax.random.split(key, 48))
    f32 = jnp.float32

    def w(shape, fan_in, scale=1.0):
        return jax.random.normal(next(ks), shape, f32) * (scale * fan_in ** -0.5)

    def gain(shape):
        return 1.0 + 0.02 * jax.random.normal(next(ks), shape, f32)

    def bias(shape):
        return 0.02 * jax.random.normal(next(ks), shape, f32)

    n_a, n_b, n_c, n_d = (_mixer_count(m) for m in range(N_MIXERS))
    d = D_MODEL
    x = jax.random.normal(next(ks), (BATCH, SEQ, d), f32)
    positions = (jax.random.randint(next(ks), (BATCH, 1), 0, 1024, jnp.int32)
                 + jnp.arange(SEQ, dtype=jnp.int32)[None, :])
    ln_g = gain((DEPTH, 3, d))
    ln_b = bias((DEPTH, 3, d))
    ffn_w_gate = w((DEPTH, 2, d, FFN_DIM), d)
    ffn_w_up = w((DEPTH, 2, d, FFN_DIM), d)
    ffn_w_down = w((DEPTH, 2, FFN_DIM, d), FFN_DIM, DN_BETA)

    conv_w_in = w((n_a, d, 2 * d), d)
    conv_w_dw = w((n_a, CONF_KERNEL, d), CONF_KERNEL)
    conv_ln_g = gain((n_a, d))
    conv_ln_b = bias((n_a, d))
    conv_w_out = w((n_a, d, d), d, DN_BETA)

    sc_w_in = w((n_b, d, 3 * d), d)
    sc_w_conv = w((n_b, SCONV_KERNEL, d), SCONV_KERNEL)
    sc_w_out = w((n_b, d, d), d, DN_BETA)

    dsa_w_dq = w((n_c, d, Q_RANK), d)
    dsa_q_norm = gain((n_c, Q_RANK))
    dsa_w_uq = w((n_c, Q_RANK, ATT_HEADS * ATT_HEAD_DIM), Q_RANK)
    dsa_w_dkv = w((n_c, d, KV_RANK), d)
    dsa_kv_norm = gain((n_c, KV_RANK))
    dsa_w_kr = w((n_c, d, ROPE_DIM), d)
    dsa_w_uk = w((n_c, ATT_HEADS, ATT_NOPE_DIM, KV_RANK), KV_RANK)
    dsa_w_uv = w((n_c, ATT_HEADS, KV_RANK, ATT_HEAD_DIM), KV_RANK)
    dsa_w_o = w((n_c, ATT_HEADS * ATT_HEAD_DIM, d), ATT_HEADS * ATT_HEAD_DIM, DN_BETA)
    dsa_w_iq = w((n_c, Q_RANK, IDX_HEADS * IDX_DIM), Q_RANK)
    dsa_w_ik = w((n_c, d, IDX_DIM), d)
    dsa_ik_ln_g = gain((n_c, IDX_DIM))
    dsa_ik_ln_b = bias((n_c, IDX_DIM))
    dsa_w_iw = w((n_c, d, IDX_HEADS), d)

    gdn_w_in = w((n_d, d, GDN_IN), d)
    gdn_w_conv = w((n_d, GDN_CONV, GDN_QKV), GDN_CONV)
    gdn_a_log = jnp.log(jax.random.uniform(next(ks), (n_d, GDN_V_HEADS), f32, 1.0, 16.0))
    dt = jnp.exp(jax.random.uniform(next(ks), (n_d, GDN_V_HEADS), f32,
                                    math.log(1e-3), math.log(1e-1)))
    gdn_dt_bias = dt + jnp.log(-jnp.expm1(-dt))
    gdn_norm_g = gain((n_d, GDN_V_DIM))
    gdn_w_out = w((n_d, GDN_V_HEADS * GDN_V_DIM, d), GDN_V_HEADS * GDN_V_DIM, DN_BETA)

    return {
        'x': x, 'positions': positions, 'ln_g': ln_g, 'ln_b': ln_b,
        'ffn_w_gate': ffn_w_gate, 'ffn_w_up': ffn_w_up, 'ffn_w_down': ffn_w_down,
        'conv_w_in': conv_w_in, 'conv_w_dw': conv_w_dw, 'conv_ln_g': conv_ln_g,
        'conv_ln_b': conv_ln_b, 'conv_w_out': conv_w_out,
        'sc_w_in': sc_w_in, 'sc_w_conv': sc_w_conv, 'sc_w_out': sc_w_out,
        'dsa_w_dq': dsa_w_dq, 'dsa_q_norm': dsa_q_norm, 'dsa_w_uq': dsa_w_uq,
        'dsa_w_dkv': dsa_w_dkv, 'dsa_kv_norm': dsa_kv_norm, 'dsa_w_kr': dsa_w_kr,
        'dsa_w_uk': dsa_w_uk, 'dsa_w_uv': dsa_w_uv, 'dsa_w_o': dsa_w_o,
        'dsa_w_iq': dsa_w_iq, 'dsa_w_ik': dsa_w_ik, 'dsa_ik_ln_g': dsa_ik_ln_g,
        'dsa_ik_ln_b': dsa_ik_ln_b, 'dsa_w_iw': dsa_w_iw,
        'gdn_w_in': gdn_w_in, 'gdn_w_conv': gdn_w_conv, 'gdn_a_log': gdn_a_log,
        'gdn_dt_bias': gdn_dt_bias, 'gdn_norm_g': gdn_norm_g, 'gdn_w_out': gdn_w_out,
    }


def reference(x, positions, ln_g, ln_b, ffn_w_gate, ffn_w_up, ffn_w_down,
              conv_w_in, conv_w_dw, conv_ln_g, conv_ln_b, conv_w_out,
              sc_w_in, sc_w_conv, sc_w_out,
              dsa_w_dq, dsa_q_norm, dsa_w_uq, dsa_w_dkv, dsa_kv_norm, dsa_w_kr,
              dsa_w_uk, dsa_w_uv, dsa_w_o, dsa_w_iq, dsa_w_ik, dsa_ik_ln_g, dsa_ik_ln_b, dsa_w_iw,
              gdn_w_in, gdn_w_conv, gdn_a_log, gdn_dt_bias, gdn_norm_g, gdn_w_out):
    cos, sin = rope_tables(positions)
    for i in range(DEPTH):
        m, j = i % N_MIXERS, i // N_MIXERS
        x = layer_norm(DN_ALPHA * x + 0.5 * swiglu(x, ffn_w_gate[i, 0], ffn_w_up[i, 0], ffn_w_down[i, 0]),
                       ln_g[i, 0], ln_b[i, 0])
        if m == 0:
            mix = conformer_conv(x, conv_w_in[j], conv_w_dw[j], conv_ln_g[j], conv_ln_b[j], conv_w_out[j])
        elif m == 1:
            mix = short_gated_conv(x, sc_w_in[j], sc_w_conv[j], sc_w_out[j])
        elif m == 2:
            mix = dsa_attention(x, cos, sin, dsa_w_dq[j], dsa_q_norm[j], dsa_w_uq[j], dsa_w_dkv[j],
                                dsa_kv_norm[j], dsa_w_kr[j], dsa_w_uk[j], dsa_w_uv[j], dsa_w_o[j],
                                dsa_w_iq[j], dsa_w_ik[j], dsa_ik_ln_g[j], dsa_ik_ln_b[j], dsa_w_iw[j])
        else:
            mix = gated_deltanet(x, gdn_w_in[j], gdn_w_conv[j], gdn_a_log[j], gdn_dt_bias[j],
                                 gdn_norm_g[j], gdn_w_out[j])
        x = layer_norm(DN_ALPHA * x + mix, ln_g[i, 1], ln_b[i, 1])
        x = layer_norm(DN_ALPHA * x + 0.5 * swiglu(x, ffn_w_gate[i, 1], ffn_w_up[i, 1], ffn_w_down[i, 1]),
                       ln_g[i, 2], ln_b[i, 2])
    return x
```

```python
import functools

import jax
import jax.numpy as jnp
from jax import lax
from jax.experimental import pallas as pl
from jax.experimental.pallas import tpu as pltpu

D_MODEL = 1024
DEPTH = 4
N_MIXERS = 4
FFN_DIM = ((8 * D_MODEL // 3 + 127) // 128) * 128
DN_ALPHA = (2.0 * DEPTH) ** 0.25
ROPE_THETA = 500000.0
LN_EPS = 1e-5
RMS_EPS = 1e-6
CONF_KERNEL = 31
SCONV_KERNEL = 3

BF16 = jnp.bfloat16
F32 = jnp.float32

V7X_VMEM_LIMIT_BYTES = 56 * 1024 * 1024
SUBLANES = 8


def _cparams(*sem):
    return pltpu.CompilerParams(dimension_semantics=sem, vmem_limit_bytes=V7X_VMEM_LIMIT_BYTES)


def _layer_norm(y, g, b):
    mu = jnp.mean(y, -1, keepdims=True)
    yc = y - mu
    var = jnp.mean(yc * yc, -1, keepdims=True)
    return yc * lax.rsqrt(var + LN_EPS) * g + b


def _silu(h):
    return h * jax.nn.sigmoid(h)


def _dot(a, b):
    return jnp.dot(a, b, preferred_element_type=F32)


def _const_spec(shape):
    nd = len(shape)
    return pl.BlockSpec(shape, lambda *_: (0,) * nd, pipeline_mode=pl.Buffered(1))


FFN_TM = 512
FFN_CHUNK = FFN_DIM // 2


def _ffn_kernel(x_ref, wg_ref, wu_ref, wd_ref, g_ref, b_ref, o_ref):
    x = x_ref[...]
    xb = x.astype(BF16)
    acc = jnp.zeros(x.shape, F32)
    for c0 in range(0, FFN_DIM, FFN_CHUNK):
        h = _dot(xb, wg_ref[:, c0:c0 + FFN_CHUNK])
        u = _dot(xb, wu_ref[:, c0:c0 + FFN_CHUNK])
        a = (_silu(h) * u).astype(BF16)
        acc = acc + _dot(a, wd_ref[c0:c0 + FFN_CHUNK, :])
    o_ref[...] = _layer_norm(DN_ALPHA * x + 0.5 * acc, g_ref[...], b_ref[...])


def _ffn_ln(x2, wg, wu, wd, g, b):
    t, d = x2.shape
    row = pl.BlockSpec((FFN_TM, d), lambda i: (i, 0))
    return pl.pallas_call(
        _ffn_kernel,
        grid=(t // FFN_TM,),
        in_specs=[row, _const_spec(wg.shape), _const_spec(wu.shape), _const_spec(wd.shape),
                  _const_spec((1, d)), _const_spec((1, d))],
        out_specs=row,
        out_shape=jax.ShapeDtypeStruct((t, d), F32),
        compiler_params=_cparams("parallel"),
        name="ffn_ln",
    )(x2, wg.astype(BF16), wu.astype(BF16), wd.astype(BF16), g.reshape(1, d), b.reshape(1, d))


def _causal_dwconv(buf_ref, u, w_ref, width, halo, ts, first_tile):
    c = u.shape[-1]

    @pl.when(first_tile)
    def _():
        buf_ref[0:halo, :] = jnp.zeros((halo, c), F32)

    @pl.when(jnp.logical_not(first_tile))
    def _():
        buf_ref[0:halo, :] = buf_ref[ts:ts + halo, :]

    buf_ref[halo:halo + ts, :] = u
    off = halo - (width - 1)
    acc = w_ref[width - 1:width, :] * u
    for k in range(width - 1):
        acc = acc + w_ref[k:k + 1, :] * buf_ref[off + k:off + k + ts, :]
    return acc


CONV_TS = 512
CONF_HALO = 32


def _conf_kernel(x_ref, win_ref, wdw_ref, lng_ref, lnb_ref, wout_ref, g_ref, b_ref, o_ref, buf_ref):
    d = D_MODEL
    x = x_ref[0]
    uu = _dot(x.astype(BF16), win_ref[...])
    u = uu[:, :d] * jax.nn.sigmoid(uu[:, d:])
    cv = _causal_dwconv(buf_ref, u, wdw_ref, CONF_KERNEL, CONF_HALO, CONV_TS, pl.program_id(1) == 0)
    v = _silu(_layer_norm(cv, lng_ref[...], lnb_ref[...]))
    mix = _dot(v.astype(BF16), wout_ref[...])
    o_ref[0] = _layer_norm(DN_ALPHA * x + mix, g_ref[...], b_ref[...])


def _conformer_ln(x, w_in, w_dw, ln_g, ln_b, w_out, g, b):
    bn, s, d = x.shape
    tile = pl.BlockSpec((1, CONV_TS, d), lambda i, j: (i, j, 0))
    return pl.pallas_call(
        _conf_kernel,
        grid=(bn, s // CONV_TS),
        in_specs=[tile, _const_spec(w_in.shape), _const_spec(w_dw.shape), _const_spec((1, d)),
                  _const_spec((1, d)), _const_spec(w_out.shape), _const_spec((1, d)), _const_spec((1, d))],
        out_specs=tile,
        out_shape=jax.ShapeDtypeStruct(x.shape, F32),
        scratch_shapes=[pltpu.VMEM((CONF_HALO + CONV_TS, d), F32)],
        compiler_params=_cparams("parallel", "arbitrary"),
        name="conformer_ln",
    )(x, w_in.astype(BF16), w_dw, ln_g.reshape(1, d), ln_b.reshape(1, d), w_out.astype(BF16),
      g.reshape(1, d), b.reshape(1, d))


SCONV_HALO = SUBLANES


def _sconv_kernel(x_ref, win_ref, wc_ref, wout_ref, g_ref, b_ref, o_ref, buf_ref):
    d = D_MODEL
    x = x_ref[0]
    bch = _dot(x.astype(BF16), win_ref[...])
    cv = _causal_dwconv(buf_ref, bch[:, d:2 * d] * bch[:, 2 * d:], wc_ref, SCONV_KERNEL, SCONV_HALO,
                        CONV_TS, pl.program_id(1) == 0)
    y = bch[:, :d] * cv
    mix = _dot(y.astype(BF16), wout_ref[...])
    o_ref[0] = _layer_norm(DN_ALPHA * x + mix, g_ref[...], b_ref[...])


def _sconv_ln(x, w_in, w_conv, w_out, g, b):
    bn, s, d = x.shape
    tile = pl.BlockSpec((1, CONV_TS, d), lambda i, j: (i, j, 0))
    return pl.pallas_call(
        _sconv_kernel,
        grid=(bn, s // CONV_TS),
        in_specs=[tile, _const_spec(w_in.shape), _const_spec(w_conv.shape), _const_spec(w_out.shape),
                  _const_spec((1, d)), _const_spec((1, d))],
        out_specs=tile,
        out_shape=jax.ShapeDtypeStruct(x.shape, F32),
        scratch_shapes=[pltpu.VMEM((SCONV_HALO + CONV_TS, d), F32)],
        compiler_params=_cparams("parallel", "arbitrary"),
        name="sconv_ln",
    )(x, w_in.astype(BF16), w_conv, w_out.astype(BF16), g.reshape(1, d), b.reshape(1, d))


GDN_K_HEADS = D_MODEL // 128
GDN_V_HEADS = 2 * GDN_K_HEADS
GDN_DK = 128
GDN_DV = 128
GDN_CONV = 4
GDN_CHUNK = 64
GDN_NK = GDN_K_HEADS * GDN_DK
GDN_NV = GDN_V_HEADS * GDN_DV
GDN_QKV = 2 * GDN_NK + GDN_NV
GDN_TS = 512
GDN_TC = 256
GDN_HALO = SUBLANES
HIGHEST = lax.Precision.HIGHEST
NT_DIMS = (((1,), (1,)), ((), ()))
TN_DIMS = (((0,), (0,)), ((), ()))


def _hdot(a, b):
    return jnp.dot(a, b, preferred_element_type=F32, precision=HIGHEST)


def _softplus(v):
    return jnp.maximum(v, 0.0) + jnp.log1p(jnp.exp(-jnp.abs(v)))


def _gdn_in_kernel(x_ref, wqkv_ref, wconv_ref, wz_ref, wba_ref, wbat_ref, alog_r_ref, dt_r_ref, alog_c_ref,
                   dt_c_ref, q_ref, k_ref, v_ref, z_ref, bcol_ref, gcol_ref, rs_ref, egl_ref, buf_ref):
    ts = GDN_TS
    xb = x_ref[0].astype(BF16)
    first = pl.program_id(1) == 0

    def conv_silu(c0, c1):
        return _silu(_causal_dwconv(buf_ref.at[:, c0:c1], _dot(xb, wqkv_ref[:, c0:c1]), wconv_ref.at[:, c0:c1],
                                    GDN_CONV, GDN_HALO, ts, first))

    for part, (o_ref, scale) in enumerate(((q_ref, GDN_DK ** -0.5), (k_ref, 1.0))):
        qk = conv_silu(part * GDN_NK, (part + 1) * GDN_NK)
        for h in range(GDN_K_HEADS):
            hd = qk[:, h * GDN_DK:(h + 1) * GDN_DK]
            o_ref[0, :, h * GDN_DK:(h + 1) * GDN_DK] = (
                hd * (lax.rsqrt(jnp.sum(hd * hd, -1, keepdims=True) + RMS_EPS) * scale))
    for part in range(GDN_NV // GDN_NK):
        v_ref[0, :, part * GDN_NK:(part + 1) * GDN_NK] = conv_silu((2 + part) * GDN_NK, (3 + part) * GDN_NK)
    z_ref[0] = _dot(xb, wz_ref[...]).astype(z_ref.dtype)

    nh = GDN_V_HEADS
    ba = _dot(xb, wba_ref[...])
    beta_c = jax.nn.sigmoid(ba[:, :nh])
    g_c = -jnp.exp(alog_r_ref[...]) * _softplus(ba[:, nh:] + dt_r_ref[...])
    bat = lax.dot_general(wbat_ref[...], xb, NT_DIMS, preferred_element_type=F32)
    beta_r = jax.nn.sigmoid(bat[:nh])
    g_r = -jnp.exp(alog_c_ref[...]) * _softplus(bat[nh:] + dt_c_ref[...])
    ri = lax.broadcasted_iota(jnp.int32, (ts, ts), 0)
    ci = lax.broadcasted_iota(jnp.int32, (ts, ts), 1)
    same = (ri // GDN_CHUNK) == (ci // GDN_CHUNK)
    low = jnp.where(same & (ci <= ri), 1.0, 0.0).astype(F32)
    upp = jnp.where(same & (ri <= ci), 1.0, 0.0).astype(F32)
    gc_c = _hdot(low, g_c)
    gc_r = _hdot(g_r, upp)
    bcol_ref[0] = beta_c
    gcol_ref[0] = gc_c
    rs_ref[0, 0:nh, :] = beta_r
    rs_ref[0, nh:2 * nh, :] = gc_r
    nchunk = ts // GDN_CHUNK
    cr = lax.broadcasted_iota(jnp.int32, (nchunk, ts), 0)
    ct = lax.broadcasted_iota(jnp.int32, (nchunk, ts), 1)
    chunk_sum = jnp.where(ct // GDN_CHUNK == cr, 1.0, 0.0).astype(F32)
    gl = _hdot(chunk_sum, g_c)
    er = lax.broadcasted_iota(jnp.int32, (nh, GDN_NV), 0)
    ec = lax.broadcasted_iota(jnp.int32, (nh, GDN_NV), 1)
    expand = jnp.where(ec // GDN_DV == er, 1.0, 0.0).astype(F32)
    egl_ref[0] = jnp.exp(_hdot(gl, expand))


def _unit_lower_inverse(a, eye):
    n = a.shape[0]
    t = eye - a
    p = _hdot(a, a)
    steps = max(n.bit_length() - 2, 0)
    for _ in range(steps - 1):
        tp = _hdot(jnp.concatenate([t, p], axis=0), p)
        t = t + tp[:n]
        p = tp[n:]
    return t + _hdot(t, p)


def _gdn_pre_kernel(q_ref, k_ref, v_ref, bcol_ref, gcol_ref, rs_ref, w_ref, u_ref, qh_ref, kt_ref, in_ref):
    c = GDN_CHUNK
    nh = GDN_V_HEADS
    j = pl.program_id(1)
    t = pl.program_id(2)
    lane = lax.broadcasted_iota(jnp.int32, (c, nh), 1)
    ri = lax.broadcasted_iota(jnp.int32, (c, c), 0)
    ci = lax.broadcasted_iota(jnp.int32, (c, c), 1)
    tril = ri >= ci
    strict = ri > ci
    eye = jnp.where(ri == ci, 1.0, 0.0).astype(F32)
    for cc in range(GDN_TC // c):
        r0 = cc * c
        q = q_ref[0, r0:r0 + c, :]
        k = k_ref[0, r0:r0 + c, :]
        qb = q.astype(BF16)
        kb = k.astype(BF16)
        kk = lax.dot_general(kb, kb, NT_DIMS, preferred_element_type=F32)
        qk = lax.dot_general(qb, kb, NT_DIMS, preferred_element_type=F32)
        chunk = t * (GDN_TC // c) + cc
        intra_pair = []
        for hh in range(2):
            h = 2 * j + hh
            sel = lane == h
            beta_c = jnp.sum(jnp.where(sel, bcol_ref[0, r0:r0 + c, :], 0.0), -1, keepdims=True)
            gc_c = jnp.sum(jnp.where(sel, gcol_ref[0, r0:r0 + c, :], 0.0), -1, keepdims=True)
            beta_r = rs_ref[0, pl.ds(h, 1), pl.ds(chunk, 1), :].reshape(1, c)
            gc_r = rs_ref[0, pl.ds(nh + h, 1), pl.ds(chunk, 1), :].reshape(1, c)
            decay = jnp.exp(jnp.where(tril, gc_c - gc_r, -jnp.inf))
            a = jnp.where(strict, beta_c * kk * decay, 0.0)
            tm = _unit_lower_inverse(a, eye)
            vh = v_ref[0, r0:r0 + c, hh * GDN_DV:(hh + 1) * GDN_DV].astype(BF16)
            u_ref[0, r0:r0 + c, hh * GDN_DV:(hh + 1) * GDN_DV] = _dot((tm * beta_r).astype(BF16), vh)
            w_ref[0, r0:r0 + c, hh * GDN_DK:(hh + 1) * GDN_DK] = _dot(
                (tm * (beta_r * jnp.exp(gc_r))).astype(BF16), kb).astype(BF16)
            qh_ref[0, r0:r0 + c, hh * GDN_DK:(hh + 1) * GDN_DK] = (q * jnp.exp(gc_c)).astype(BF16)
            kt_ref[0, r0:r0 + c, hh * GDN_DK:(hh + 1) * GDN_DK] = (
                k * jnp.exp(gc_c[c - 1:c, :] - gc_c)).astype(BF16)
            intra_pair.append(jnp.where(tril, qk * decay, 0.0).astype(BF16))
        in_ref[0, r0:r0 + c, :] = jnp.concatenate(intra_pair, axis=1)


def _gdn_scan_kernel(w_ref, u_ref, qh_ref, kt_ref, in_ref, egl_ref, o_ref, s_ref):
    c = GDN_CHUNK
    pw = 2 * GDN_DV
    ri = lax.broadcasted_iota(jnp.int32, (pw, pw), 0)
    ci = lax.broadcasted_iota(jnp.int32, (pw, pw), 1)
    block_diag = (ri // GDN_DK) == (ci // GDN_DV)

    @pl.when(pl.program_id(1) == 0)
    def _():
        s_ref[...] = jnp.zeros(s_ref.shape, F32)

    def step(n, carry):
        r0 = pl.multiple_of(n * c, c)
        rows = pl.ds(r0, c)
        for p in range(GDN_K_HEADS):
            cols = slice(p * pw, (p + 1) * pw)
            s = s_ref[p]
            wq = jnp.concatenate([w_ref[0, rows, cols], qh_ref[0, rows, cols]], axis=0)
            r = _dot(wq, s.astype(BF16))
            vn = (u_ref[0, rows, cols] - r[:c]).astype(BF16)
            zero = jnp.zeros((c, GDN_DV), BF16)
            vbd = jnp.concatenate([jnp.concatenate([vn[:, :GDN_DV], zero], axis=1),
                                   jnp.concatenate([zero, vn[:, GDN_DV:]], axis=1)], axis=0)
            o_ref[0, rows, cols] = r[c:] + _dot(in_ref[0, rows, p * 2 * c:(p + 1) * 2 * c], vbd)
            upd = lax.dot_general(kt_ref[0, rows, cols], vn, TN_DIMS, preferred_element_type=F32)
            s_ref[p] = s * egl_ref[0, pl.ds(n, 1), cols] + jnp.where(block_diag, upd, 0.0)
        return carry

    lax.fori_loop(0, GDN_TS // c, step, 0)


def _gdn_out_kernel(o_ref, z_ref, x_ref, ng_ref, wout_ref, g_ref, b_ref, y_ref):
    parts = []
    for h in range(GDN_V_HEADS):
        cols = slice(h * GDN_DV, (h + 1) * GDN_DV)
        oh = o_ref[0, :, cols]
        oh = oh * lax.rsqrt(jnp.mean(oh * oh, -1, keepdims=True) + RMS_EPS) * ng_ref[...]
        parts.append((oh * _silu(z_ref[0, :, cols].astype(F32))).astype(BF16))
    mix = _dot(jnp.concatenate(parts, axis=1), wout_ref[...])
    y_ref[0] = _layer_norm(DN_ALPHA * x_ref[0] + mix, g_ref[...], b_ref[...])


def _gdn_ln(x, w_in, w_conv, a_log, dt_bias, norm_g, w_out, g, b):
    bn, s, d = x.shape
    nh, ts = GDN_V_HEADS, GDN_TS
    nchunks = s // GDN_CHUNK
    w_qkv = w_in[:, :GDN_QKV].astype(BF16)
    w_z = w_in[:, GDN_QKV:GDN_QKV + GDN_NV].astype(BF16)
    w_ba = w_in[:, GDN_QKV + GDN_NV:].astype(BF16)
    tile = lambda w: pl.BlockSpec((1, ts, w), lambda i, jj: (i, jj, 0))
    q, k, v, z, bcol, gcol, rs, egl = pl.pallas_call(
        _gdn_in_kernel,
        grid=(bn, s // ts),
        in_specs=[tile(d), _const_spec(w_qkv.shape), _const_spec(w_conv.shape), _const_spec(w_z.shape),
                  _const_spec(w_ba.shape), _const_spec((2 * nh, d)), _const_spec((1, nh)), _const_spec((1, nh)),
                  _const_spec((nh, 1)), _const_spec((nh, 1))],
        out_specs=[tile(GDN_NK), tile(GDN_NK), tile(GDN_NV), tile(GDN_NV), tile(nh), tile(nh),
                   pl.BlockSpec((1, 2 * nh, ts), lambda i, jj: (i, 0, jj)),
                   pl.BlockSpec((1, ts // GDN_CHUNK, GDN_NV), lambda i, jj: (i, jj, 0))],
        out_shape=[jax.ShapeDtypeStruct((bn, s, GDN_NK), F32), jax.ShapeDtypeStruct((bn, s, GDN_NK), F32),
                   jax.ShapeDtypeStruct((bn, s, GDN_NV), F32), jax.ShapeDtypeStruct((bn, s, GDN_NV), BF16),
                   jax.ShapeDtypeStruct((bn, s, nh), F32), jax.ShapeDtypeStruct((bn, s, nh), F32),
                   jax.ShapeDtypeStruct((bn, 2 * nh, s), F32),
                   jax.ShapeDtypeStruct((bn, nchunks, GDN_NV), F32)],
        scratch_shapes=[pltpu.VMEM((GDN_HALO + ts, GDN_QKV), F32)],
        compiler_params=_cparams("parallel", "arbitrary"),
        name="gdn_in",
    )(x, w_qkv, w_conv, w_z, w_ba, w_ba.T, a_log.reshape(1, nh), dt_bias.reshape(1, nh),
      a_log.reshape(nh, 1), dt_bias.reshape(nh, 1))

    rs4 = rs.reshape(bn, 2 * nh, nchunks, GDN_CHUNK)
    tc = GDN_TC
    pair = lambda w: pl.BlockSpec((1, tc, w), lambda i, jj, tt: (i, tt, jj))
    whole = lambda w: pl.BlockSpec((1, tc, w), lambda i, jj, tt: (i, tt, 0))
    w, u, qh, kt, intra = pl.pallas_call(
        _gdn_pre_kernel,
        grid=(bn, GDN_K_HEADS, s // tc),
        in_specs=[pair(GDN_DK), pair(GDN_DK), pair(2 * GDN_DV), whole(nh), whole(nh),
                  pl.BlockSpec((1, 2 * nh, nchunks, GDN_CHUNK), lambda i, jj, tt: (i, 0, 0, 0))],
        out_specs=[pair(2 * GDN_DK), pair(2 * GDN_DV), pair(2 * GDN_DK), pair(2 * GDN_DK), pair(2 * GDN_CHUNK)],
        out_shape=[jax.ShapeDtypeStruct((bn, s, 2 * GDN_NK), BF16), jax.ShapeDtypeStruct((bn, s, GDN_NV), F32),
                   jax.ShapeDtypeStruct((bn, s, 2 * GDN_NK), BF16), jax.ShapeDtypeStruct((bn, s, 2 * GDN_NK), BF16),
                   jax.ShapeDtypeStruct((bn, s, GDN_V_HEADS * GDN_CHUNK), BF16)],
        compiler_params=_cparams("parallel", "parallel", "parallel"),
        name="gdn_pre",
    )(q, k, v, bcol, gcol, rs4)

    o = pl.pallas_call(
        _gdn_scan_kernel,
        grid=(bn, s // ts),
        in_specs=[tile(2 * GDN_NK), tile(GDN_NV), tile(2 * GDN_NK), tile(2 * GDN_NK),
                  tile(GDN_V_HEADS * GDN_CHUNK),
                  pl.BlockSpec((1, ts // GDN_CHUNK, GDN_NV), lambda i, jj: (i, jj, 0))],
        out_specs=tile(GDN_NV),
        out_shape=jax.ShapeDtypeStruct((bn, s, GDN_NV), F32),
        scratch_shapes=[pltpu.VMEM((GDN_K_HEADS, 2 * GDN_DK, 2 * GDN_DV), F32)],
        compiler_params=_cparams("parallel", "arbitrary"),
        name="gdn_scan",
    )(w, u, qh, kt, intra, egl)

    return pl.pallas_call(
        _gdn_out_kernel,
        grid=(bn, s // ts),
        in_specs=[tile(GDN_NV), tile(GDN_NV), tile(d), _const_spec((1, GDN_DV)), _const_spec(w_out.shape),
                  _const_spec((1, d)), _const_spec((1, d))],
        out_specs=tile(d),
        out_shape=jax.ShapeDtypeStruct(x.shape, F32),
        compiler_params=_cparams("parallel", "parallel"),
        name="gdn_out",
    )(o, z, x, norm_g.reshape(1, GDN_DV), w_out.astype(BF16), g.reshape(1, d), b.reshape(1, d))


ATT_HEADS = D_MODEL // 64
ATT_HEAD_DIM = 64
ROPE_DIM = ATT_HEAD_DIM // 4
ROPE_HALF = ROPE_DIM // 2
ATT_NOPE_DIM = ATT_HEAD_DIM - ROPE_DIM
Q_RANK = D_MODEL // 4
KV_RANK = D_MODEL // 8
IDX_HEADS = 8
IDX_DIM = ATT_HEAD_DIM
TOPK_MAX = 256
LANES = 128
QK_WIDTH = 2 * LANES
DSA_TS = 512
DSA_QB = 128
DSA_KT = 512
MASKED = -1e30


def _rope_tab_kernel(pos_ref, inv_ref, rot_ref, sgn_ref, cos_ref, sin_ref):
    ang = pos_ref[0].astype(F32) * inv_ref[...]
    cos_ref[0] = jnp.where(rot_ref[...] > 0.0, jnp.cos(ang), 1.0)
    sin_ref[0] = jnp.sin(ang) * sgn_ref[...]


def _rope_tables(positions):
    bn, s = positions.shape
    lane = jnp.arange(LANES)
    inv = ROPE_THETA ** (-jnp.arange(0, ROPE_DIM, 2, dtype=F32) / ROPE_DIM)
    within = lane % ATT_HEAD_DIM
    rot = (within < ROPE_DIM).astype(F32)
    sgn = jnp.where(within < ROPE_HALF, -1.0, 1.0).astype(F32) * rot
    row = lambda v: v.reshape(1, LANES)
    ts = DSA_TS
    tile = pl.BlockSpec((1, ts, LANES), lambda i, j: (i, j, 0))
    return pl.pallas_call(
        _rope_tab_kernel,
        grid=(bn, s // ts),
        in_specs=[pl.BlockSpec((1, ts, 1), lambda i, j: (i, j, 0))] + [_const_spec((1, LANES))] * 3,
        out_specs=[tile, tile],
        out_shape=[jax.ShapeDtypeStruct((bn, s, LANES), F32)] * 2,
        compiler_params=_cparams("parallel", "parallel"),
        name="rope_tables",
    )(positions.reshape(bn, s, 1), row(inv[lane % ROPE_HALF]), row(rot), row(sgn))


def _rope(v, cosm, sinm):
    lane = lax.broadcasted_iota(jnp.int32, (1, LANES), 1)
    low = (lane % ROPE_DIM) < ROPE_HALF
    cols = []
    for c0 in range(0, v.shape[1], LANES):
        blk = v[:, c0:c0 + LANES]
        partner = jnp.where(low, pltpu.roll(blk, LANES - ROPE_HALF, axis=1), pltpu.roll(blk, ROPE_HALF, axis=1))
        cols.append(blk * cosm + partner * sinm)
    return cols[0] if len(cols) == 1 else jnp.concatenate(cols, axis=1)


def _rms_norm(v, g):
    return v * lax.rsqrt(jnp.mean(v * v, -1, keepdims=True) + RMS_EPS) * g


def _dsa_proj_kernel(x_ref, cos_ref, sin_ref, wdq_ref, qn_ref, wuq_ref, wcomb_ref, wdkv_ref, kvn_ref, wkr_ref,
                     wiq_ref, wik_ref, ikg_ref, ikb_ref, prot_ref, wiwt_ref,
                     q2_ref, k2_ref, qi_ref, ki_ref, wit_ref):
    xb = x_ref[0].astype(BF16)
    cosm, sinm = cos_ref[0], sin_ref[0]
    cq = _rms_norm(_dot(xb, wdq_ref[...]), qn_ref[...]).astype(BF16)
    q = _rope(_dot(cq, wuq_ref[...]), cosm, sinm) * ATT_HEAD_DIM ** -0.5
    q2_ref[0] = _dot(q.astype(BF16), wcomb_ref[...]).astype(BF16)
    ckv = _rms_norm(_dot(xb, wdkv_ref[...]), kvn_ref[...])
    kr = _rope(_dot(xb, wkr_ref[...]), cosm, sinm)
    k2_ref[0] = jnp.concatenate([ckv, kr], axis=1).astype(BF16)
    qi = _rope(_dot(cq, wiq_ref[...]), cosm, sinm).astype(BF16)
    for h in range(IDX_HEADS):
        qi_ref[0, h] = qi[:, h * IDX_DIM:(h + 1) * IDX_DIM]
    kin = _layer_norm(_dot(xb, wik_ref[...]), ikg_ref[...], ikb_ref[...])
    ki = kin * cosm[:, :IDX_DIM] + _hdot(kin, prot_ref[...]) * sinm[:, :IDX_DIM]
    ki_ref[0] = ki.astype(BF16)
    wit = lax.dot_general(wiwt_ref[...], xb, NT_DIMS, preferred_element_type=F32)
    wit_ref[0] = wit * (IDX_HEADS * IDX_DIM) ** -0.5


def _dsa_attn_kernel(q2_ref, k2_ref, qi_ref, ki_ref, wit_ref, o_ref, key_ref, mask_ref):
    qb, kt_sz = DSA_QB, DSA_KT
    i = pl.program_id(1)
    n_tiles = (i * qb) // kt_sz + 1
    q_pos = i * qb + lax.broadcasted_iota(jnp.int32, (1, qb), 1)

    def score_tile(t, carry):
        r0 = pl.multiple_of(t * kt_sz, kt_sz)
        kblk = ki_ref[0, pl.ds(r0, kt_sz), :]
        acc = jnp.zeros((kt_sz, qb), F32)
        for h in range(IDX_HEADS):
            sc = lax.dot_general(kblk, qi_ref[0, h], NT_DIMS, preferred_element_type=F32)
            acc = acc + jnp.maximum(sc, 0.0) * wit_ref[0, h:h + 1, :]
        k_pos = r0 + lax.broadcasted_iota(jnp.int32, (kt_sz, 1), 0)
        acc = jnp.where(k_pos <= q_pos, acc, -jnp.inf)
        bits = pltpu.bitcast(acc, jnp.int32)
        key_ref[pl.ds(r0, kt_sz), :] = bits ^ ((bits >> 31) & jnp.int32(0x7FFFFFFF))
        return carry

    lax.fori_loop(0, n_tiles, score_tile, 0)

    def count(pred):
        def body(t, acc):
            r0 = pl.multiple_of(t * kt_sz, kt_sz)
            k_pos = r0 + lax.broadcasted_iota(jnp.int32, (kt_sz, 1), 0)
            hit = jnp.where(pred(key_ref[pl.ds(r0, kt_sz), :], k_pos), 1, 0).astype(jnp.int32)
            return acc + jnp.sum(hit.reshape(kt_sz // SUBLANES, SUBLANES, qb), axis=0)
        acc = lax.fori_loop(0, n_tiles, body, jnp.zeros((SUBLANES, qb), jnp.int32))
        return jnp.sum(acc, axis=0, keepdims=True)

    int_min = jnp.int32(-2 ** 31)
    topk = jnp.int32(TOPK_MAX)

    def search_bit(b, thr):
        cand = thr + lax.shift_left(jnp.int32(1), jnp.int32(31) - b)
        return jnp.where(count(lambda kb, kp: kb >= cand) >= topk, cand, thr)

    few_keys = (i + 1) * qb <= TOPK_MAX
    thr0 = jnp.full((1, qb), int_min, jnp.int32)
    thr = lax.cond(few_keys, lambda: thr0, lambda: lax.fori_loop(0, 32, search_bit, thr0))

    need = topk - count(lambda kb, kp: kb > thr)
    n_eq = count(lambda kb, kp: kb == thr)
    full_cut = jnp.full((1, qb), jnp.int32(2 ** 30), jnp.int32)

    def search_cut():
        def cut_bit(b, cut):
            cand = cut + lax.shift_left(jnp.int32(1), jnp.int32(12) - b)
            below = count(lambda kb, kp: (kb == thr) & (kp < cand))
            return jnp.where(below < need, cand, cut)
        return lax.fori_loop(0, 13, cut_bit, jnp.zeros((1, qb), jnp.int32))

    tied = jnp.logical_and(jnp.logical_not(few_keys), jnp.max(jnp.where(n_eq != need, 1, 0)) > 0)
    cut = lax.cond(tied, search_cut, lambda: full_cut)

    def mask_tile(t, carry):
        r0 = pl.multiple_of(t * kt_sz, kt_sz)
        kb = key_ref[pl.ds(r0, kt_sz), :]
        k_pos = r0 + lax.broadcasted_iota(jnp.int32, (kt_sz, 1), 0)
        keep = ((kb > thr) | ((kb == thr) & (k_pos <= cut))) & (k_pos <= q_pos)
        mask_ref[:, pl.ds(r0, kt_sz)] = jnp.where(keep, 0.0, MASKED).astype(F32).T
        return carry

    lax.fori_loop(0, n_tiles, mask_tile, 0)

    nh = ATT_HEADS
    q2 = jnp.concatenate([q2_ref[0, :, h * QK_WIDTH:(h + 1) * QK_WIDTH] for h in range(nh)], axis=0)

    def attn_tile(t, carry):
        m, l, acc = carry
        r0 = pl.multiple_of(t * kt_sz, kt_sz)
        k2 = k2_ref[0, pl.ds(r0, kt_sz), :]
        logits = lax.dot_general(q2, k2, NT_DIMS, preferred_element_type=F32)
        logits = (logits.reshape(nh, qb, kt_sz) + mask_ref[:, pl.ds(r0, kt_sz)][None]).reshape(nh * qb, kt_sz)
        m_new = jnp.maximum(m, jnp.max(logits, -1, keepdims=True))
        alpha = jnp.exp(m - m_new)
        p = jnp.exp(logits - m_new)
        l_new = alpha * l + jnp.sum(p, -1, keepdims=True)
        acc_new = alpha * acc + _dot(p.astype(BF16), k2[:, :KV_RANK])
        return m_new, l_new, acc_new

    m0 = jnp.full((nh * qb, 1), MASKED, F32)
    l0 = jnp.zeros((nh * qb, 1), F32)
    acc0 = jnp.zeros((nh * qb, KV_RANK), F32)
    _, l, acc = lax.fori_loop(0, n_tiles, attn_tile, (m0, l0, acc0))
    o = acc / l
    for h in range(nh):
        o_ref[0, :, h * KV_RANK:(h + 1) * KV_RANK] = o[h * qb:(h + 1) * qb].astype(o_ref.dtype)


def _dsa_out_kernel(o_ref, x_ref, wuv_ref, wo_ref, g_ref, b_ref, y_ref):
    o = _dot(o_ref[0], wuv_ref[...]).astype(BF16)
    mix = _dot(o, wo_ref[...])
    y_ref[0] = _layer_norm(DN_ALPHA * x_ref[0] + mix, g_ref[...], b_ref[...])


def _block_diag(blocks):
    n, r, c = blocks.shape
    idx = jnp.arange(n)
    return jnp.zeros((n, r, n, c), blocks.dtype).at[idx, :, idx, :].set(blocks).reshape(n * r, n * c)


def _dsa_ln(x, positions, w_dq, q_norm, w_uq, w_dkv, kv_norm, w_kr, w_uk, w_uv, w_o,
            w_iq, w_ik, ik_g, ik_b, w_iw, g, b):
    bn, s, d = x.shape
    ts, nh = DSA_TS, ATT_HEADS
    cosm, sinm = _rope_tables(positions)

    head_map = jnp.zeros((nh, ATT_HEAD_DIM, QK_WIDTH), F32)
    head_map = head_map.at[:, ROPE_DIM:, :KV_RANK].set(w_uk)
    head_map = head_map.at[:, :ROPE_DIM, KV_RANK:KV_RANK + ROPE_DIM].set(jnp.eye(ROPE_DIM, dtype=F32))
    w_comb = _block_diag(head_map).astype(BF16)
    w_kr_pad = jnp.zeros((d, LANES), F32).at[:, :ROPE_DIM].set(w_kr).astype(BF16)
    lane = jnp.arange(IDX_DIM)
    partner = jnp.where(lane < ROPE_HALF, lane + ROPE_HALF, lane - ROPE_HALF)
    p_rot = ((lane[:, None] == partner[None, :]) & (lane[None, :] < ROPE_DIM)).astype(F32)

    tile = lambda w: pl.BlockSpec((1, ts, w), lambda i, j: (i, j, 0))
    consts = [w_dq.astype(BF16), q_norm.reshape(1, -1), w_uq.astype(BF16), w_comb, w_dkv.astype(BF16),
              kv_norm.reshape(1, -1), w_kr_pad, w_iq.astype(BF16), w_ik.astype(BF16), ik_g.reshape(1, -1),
              ik_b.reshape(1, -1), p_rot, w_iw.T.astype(BF16)]
    q2, k2, qi, ki, wit = pl.pallas_call(
        _dsa_proj_kernel,
        grid=(bn, s // ts),
        in_specs=[tile(d), tile(LANES), tile(LANES)] + [_const_spec(c.shape) for c in consts],
        out_specs=[tile(nh * QK_WIDTH), tile(QK_WIDTH),
                   pl.BlockSpec((1, IDX_HEADS, ts, IDX_DIM), lambda i, j: (i, 0, j, 0)),
                   tile(IDX_DIM), pl.BlockSpec((1, IDX_HEADS, ts), lambda i, j: (i, 0, j))],
        out_shape=[jax.ShapeDtypeStruct((bn, s, nh * QK_WIDTH), BF16), jax.ShapeDtypeStruct((bn, s, QK_WIDTH), BF16),
                   jax.ShapeDtypeStruct((bn, IDX_HEADS, s, IDX_DIM), BF16),
                   jax.ShapeDtypeStruct((bn, s, IDX_DIM), BF16), jax.ShapeDtypeStruct((bn, IDX_HEADS, s), F32)],
        compiler_params=_cparams("parallel", "parallel"),
        name="dsa_proj",
    )(x, cosm, sinm, *consts)

    qb = DSA_QB
    o_lat = pl.pallas_call(
        _dsa_attn_kernel,
        grid=(bn, s // qb),
        in_specs=[pl.BlockSpec((1, qb, nh * QK_WIDTH), lambda i, j: (i, j, 0)),
                  pl.BlockSpec((1, s, QK_WIDTH), lambda i, j: (i, 0, 0)),
                  pl.BlockSpec((1, IDX_HEADS, qb, IDX_DIM), lambda i, j: (i, 0, j, 0)),
                  pl.BlockSpec((1, s, IDX_DIM), lambda i, j: (i, 0, 0)),
                  pl.BlockSpec((1, IDX_HEADS, qb), lambda i, j: (i, 0, j))],
        out_specs=pl.BlockSpec((1, qb, nh * KV_RANK), lambda i, j: (i, j, 0)),
        out_shape=jax.ShapeDtypeStruct((bn, s, nh * KV_RANK), BF16),
        scratch_shapes=[pltpu.VMEM((s, qb), jnp.int32), pltpu.VMEM((qb, s), F32)],
        compiler_params=_cparams("parallel", "parallel"),
        name="dsa_attn",
    )(q2, k2, qi, ki, wit)

    w_uv_bd = _block_diag(w_uv).astype(BF16)
    return pl.pallas_call(
        _dsa_out_kernel,
        grid=(bn, s // ts),
        in_specs=[tile(nh * KV_RANK), tile(d), _const_spec(w_uv_bd.shape), _const_spec(w_o.shape),
                  _const_spec((1, d)), _const_spec((1, d))],
        out_specs=tile(d),
        out_shape=jax.ShapeDtypeStruct(x.shape, F32),
        compiler_params=_cparams("parallel", "parallel"),
        name="dsa_out",
    )(o_lat, x, w_uv_bd, w_o.astype(BF16), g.reshape(1, d), b.reshape(1, d))


def kernel(x, positions, ln_g, ln_b, ffn_w_gate, ffn_w_up, ffn_w_down,
           conv_w_in, conv_w_dw, conv_ln_g, conv_ln_b, conv_w_out,
           sc_w_in, sc_w_conv, sc_w_out,
           dsa_w_dq, dsa_q_norm, dsa_w_uq, dsa_w_dkv, dsa_kv_norm, dsa_w_kr,
           dsa_w_uk, dsa_w_uv, dsa_w_o, dsa_w_iq, dsa_w_ik, dsa_ik_ln_g, dsa_ik_ln_b, dsa_w_iw,
           gdn_w_in, gdn_w_conv, gdn_a_log, gdn_dt_bias, gdn_norm_g, gdn_w_out):
    bn, s, d = x.shape

    def ffn(h, i, half):
        return _ffn_ln(h.reshape(bn * s, d), ffn_w_gate[i, half], ffn_w_up[i, half], ffn_w_down[i, half],
                       ln_g[i, 2 * half], ln_b[i, 2 * half]).reshape(bn, s, d)

    for i in range(DEPTH):
        m, j = i % N_MIXERS, i // N_MIXERS
        x = ffn(x, i, 0)
        g, b = ln_g[i, 1], ln_b[i, 1]
        if m == 0:
            x = _conformer_ln(x, conv_w_in[j], conv_w_dw[j], conv_ln_g[j], conv_ln_b[j], conv_w_out[j], g, b)
        elif m == 1:
            x = _sconv_ln(x, sc_w_in[j], sc_w_conv[j], sc_w_out[j], g, b)
        elif m == 2:
            x = _dsa_ln(x, positions, dsa_w_dq[j], dsa_q_norm[j], dsa_w_uq[j], dsa_w_dkv[j], dsa_kv_norm[j],
                        dsa_w_kr[j], dsa_w_uk[j], dsa_w_uv[j], dsa_w_o[j], dsa_w_iq[j], dsa_w_ik[j],
                        dsa_ik_ln_g[j], dsa_ik_ln_b[j], dsa_w_iw[j], g, b)
        else:
            x = _gdn_ln(x, gdn_w_in[j], gdn_w_conv[j], gdn_a_log[j], gdn_dt_bias[j], gdn_norm_g[j],
                        gdn_w_out[j], g, b)
        x = ffn(x, i, 1)
    return x
```

```python
import functools

import jax
import jax.numpy as jnp
from jax import lax
from jax.experimental import pallas as pl
from jax.experimental.pallas import tpu as pltpu

D_MODEL = 1024
DEPTH = 4
N_MIXERS = 4
FFN_DIM = ((8 * D_MODEL // 3 + 127) // 128) * 128
DN_ALPHA = (2.0 * DEPTH) ** 0.25
ROPE_THETA = 500000.0
LN_EPS = 1e-5
RMS_EPS = 1e-6
CONF_KERNEL = 31
SCONV_KERNEL = 3

BF16 = jnp.bfloat16
F32 = jnp.float32

V7X_VMEM_LIMIT_BYTES = 56 * 1024 * 1024
SUBLANES = 8
LANES = 128


def _cparams(*sem):
    return pltpu.CompilerParams(dimension_semantics=sem, vmem_limit_bytes=V7X_VMEM_LIMIT_BYTES)


def _layer_norm(y, g, b):
    mu = jnp.mean(y, -1, keepdims=True)
    yc = y - mu
    var = jnp.mean(yc * yc, -1, keepdims=True)
    return yc * lax.rsqrt(var + LN_EPS) * g + b


def _silu(h):
    return h * jax.nn.sigmoid(h)


def _dot(a, b):
    return jnp.dot(a, b, preferred_element_type=F32)


def _const_spec(shape):
    nd = len(shape)
    return pl.BlockSpec(shape, lambda *_: (0,) * nd, pipeline_mode=pl.Buffered(1))


FFN_TM = 512
FFN_CHUNK = 768


def _ffn_kernel(x_ref, wg_ref, wu_ref, wd_ref, g_ref, b_ref, o_ref):
    x = x_ref[...]
    xb = x.astype(BF16)
    bounds = list(range(0, FFN_DIM, FFN_CHUNK)) + [FFN_DIM]
    gate_up = lambda c: (_dot(xb, wg_ref[:, bounds[c]:bounds[c + 1]]), _dot(xb, wu_ref[:, bounds[c]:bounds[c + 1]]))
    acc = jnp.zeros(x.shape, F32)
    h, u = gate_up(0)
    for c in range(len(bounds) - 1):
        nxt = gate_up(c + 1) if c + 2 < len(bounds) else None
        a = (_silu(h) * u).astype(BF16)
        acc = acc + _dot(a, wd_ref[bounds[c]:bounds[c + 1], :])
        if nxt is not None:
            h, u = nxt
    o_ref[...] = _layer_norm(DN_ALPHA * x + 0.5 * acc, g_ref[...], b_ref[...])


def _ffn_ln(x2, wg, wu, wd, g, b):
    t, d = x2.shape
    row = pl.BlockSpec((FFN_TM, d), lambda i: (i, 0))
    return pl.pallas_call(
        _ffn_kernel,
        grid=(t // FFN_TM,),
        in_specs=[row, _const_spec(wg.shape), _const_spec(wu.shape), _const_spec(wd.shape),
                  _const_spec((1, d)), _const_spec((1, d))],
        out_specs=row,
        out_shape=jax.ShapeDtypeStruct((t, d), F32),
        compiler_params=_cparams("parallel"),
        name="ffn_ln",
    )(x2, wg.astype(BF16), wu.astype(BF16), wd.astype(BF16), g.reshape(1, d), b.reshape(1, d))


def _causal_dwconv(buf_ref, u, w_ref, width, halo, ts, first_tile):
    c = u.shape[-1]

    @pl.when(first_tile)
    def _():
        buf_ref[0:halo, :] = jnp.zeros((halo, c), F32)

    @pl.when(jnp.logical_not(first_tile))
    def _():
        buf_ref[0:halo, :] = buf_ref[ts:ts + halo, :]

    buf_ref[halo:halo + ts, :] = u
    off = halo - (width - 1)
    acc = w_ref[width - 1:width, :] * u
    for k in range(width - 1):
        acc = acc + w_ref[k:k + 1, :] * buf_ref[off + k:off + k + ts, :]
    return acc


CONV_TS = 512
CONF_HALO = 32


def _conf_kernel(x_ref, win_ref, wdw_ref, lng_ref, lnb_ref, wout_ref, g_ref, b_ref, o_ref, buf_ref):
    d = D_MODEL
    x = x_ref[0]
    uu = _dot(x.astype(BF16), win_ref[...])
    u = uu[:, :d] * jax.nn.sigmoid(uu[:, d:])
    cv = _causal_dwconv(buf_ref, u, wdw_ref, CONF_KERNEL, CONF_HALO, CONV_TS, pl.program_id(1) == 0)
    v = _silu(_layer_norm(cv, lng_ref[...], lnb_ref[...]))
    mix = _dot(v.astype(BF16), wout_ref[...])
    o_ref[0] = _layer_norm(DN_ALPHA * x + mix, g_ref[...], b_ref[...])


def _conformer_ln(x, w_in, w_dw, ln_g, ln_b, w_out, g, b):
    bn, s, d = x.shape
    tile = pl.BlockSpec((1, CONV_TS, d), lambda i, j: (i, j, 0))
    return pl.pallas_call(
        _conf_kernel,
        grid=(bn, s // CONV_TS),
        in_specs=[tile, _const_spec(w_in.shape), _const_spec(w_dw.shape), _const_spec((1, d)),
                  _const_spec((1, d)), _const_spec(w_out.shape), _const_spec((1, d)), _const_spec((1, d))],
        out_specs=tile,
        out_shape=jax.ShapeDtypeStruct(x.shape, F32),
        scratch_shapes=[pltpu.VMEM((CONF_HALO + CONV_TS, d), F32)],
        compiler_params=_cparams("parallel", "arbitrary"),
        name="conformer_ln",
    )(x, w_in.astype(BF16), w_dw, ln_g.reshape(1, d), ln_b.reshape(1, d), w_out.astype(BF16),
      g.reshape(1, d), b.reshape(1, d))


SCONV_HALO = SUBLANES


def _sconv_kernel(x_ref, win_ref, wc_ref, wout_ref, g_ref, b_ref, o_ref, buf_ref):
    d = D_MODEL
    x = x_ref[0]
    bch = _dot(x.astype(BF16), win_ref[...])
    cv = _causal_dwconv(buf_ref, bch[:, d:2 * d] * bch[:, 2 * d:], wc_ref, SCONV_KERNEL, SCONV_HALO,
                        CONV_TS, pl.program_id(1) == 0)
    y = bch[:, :d] * cv
    mix = _dot(y.astype(BF16), wout_ref[...])
    o_ref[0] = _layer_norm(DN_ALPHA * x + mix, g_ref[...], b_ref[...])


def _sconv_ln(x, w_in, w_conv, w_out, g, b):
    bn, s, d = x.shape
    tile = pl.BlockSpec((1, CONV_TS, d), lambda i, j: (i, j, 0))
    return pl.pallas_call(
        _sconv_kernel,
        grid=(bn, s // CONV_TS),
        in_specs=[tile, _const_spec(w_in.shape), _const_spec(w_conv.shape), _const_spec(w_out.shape),
                  _const_spec((1, d)), _const_spec((1, d))],
        out_specs=tile,
        out_shape=jax.ShapeDtypeStruct(x.shape, F32),
        scratch_shapes=[pltpu.VMEM((SCONV_HALO + CONV_TS, d), F32)],
        compiler_params=_cparams("parallel", "arbitrary"),
        name="sconv_ln",
    )(x, w_in.astype(BF16), w_conv, w_out.astype(BF16), g.reshape(1, d), b.reshape(1, d))


GDN_K_HEADS = D_MODEL // 128
GDN_V_HEADS = 2 * GDN_K_HEADS
GDN_DK = 128
GDN_DV = 128
GDN_CONV = 4
GDN_CHUNK = 64
GDN_NK = GDN_K_HEADS * GDN_DK
GDN_NV = GDN_V_HEADS * GDN_DV
GDN_QKV = 2 * GDN_NK + GDN_NV
GDN_TS = 512
GDN_TC = 256
GDN_TP = 512
GDN_HALO = SUBLANES
HIGHEST = lax.Precision.HIGHEST
NT_DIMS = (((1,), (1,)), ((), ()))
TN_DIMS = (((0,), (0,)), ((), ()))


def _hdot(a, b):
    return jnp.dot(a, b, preferred_element_type=F32, precision=HIGHEST)


def _softplus(v):
    return jnp.maximum(v, 0.0) + jnp.log1p(jnp.exp(-jnp.abs(v)))


def _gdn_in_kernel(x_ref, wqkv_ref, wconv_ref, wz_ref, wba_ref, wbat_ref, alog_r_ref, dt_r_ref, alog_c_ref,
                   dt_c_ref, q_ref, k_ref, v_ref, z_ref, bcol_ref, gcol_ref, rs_ref, egl_ref, buf_ref):
    ts = GDN_TS
    xb = x_ref[0].astype(BF16)
    first = pl.program_id(1) == 0

    def conv_silu(c0, c1):
        return _silu(_causal_dwconv(buf_ref.at[:, c0:c1], _dot(xb, wqkv_ref[:, c0:c1]), wconv_ref.at[:, c0:c1],
                                    GDN_CONV, GDN_HALO, ts, first))

    for part, (o_ref, scale) in enumerate(((q_ref, GDN_DK ** -0.5), (k_ref, 1.0))):
        qk = conv_silu(part * GDN_NK, (part + 1) * GDN_NK)
        for h in range(GDN_K_HEADS):
            hd = qk[:, h * GDN_DK:(h + 1) * GDN_DK]
            o_ref[0, :, h * GDN_DK:(h + 1) * GDN_DK] = (
                hd * (lax.rsqrt(jnp.sum(hd * hd, -1, keepdims=True) + RMS_EPS) * scale))
    for part in range(GDN_NV // GDN_NK):
        v_ref[0, :, part * GDN_NK:(part + 1) * GDN_NK] = conv_silu((2 + part) * GDN_NK, (3 + part) * GDN_NK)
    z_ref[0] = _dot(xb, wz_ref[...]).astype(z_ref.dtype)

    nh = GDN_V_HEADS
    ba = _dot(xb, wba_ref[...])
    beta_c = jax.nn.sigmoid(ba[:, :nh])
    g_c = -jnp.exp(alog_r_ref[...]) * _softplus(ba[:, nh:] + dt_r_ref[...])
    bat = lax.dot_general(wbat_ref[...], xb, NT_DIMS, preferred_element_type=F32)
    beta_r = jax.nn.sigmoid(bat[:nh])
    g_r = -jnp.exp(alog_c_ref[...]) * _softplus(bat[nh:] + dt_c_ref[...])
    ri = lax.broadcasted_iota(jnp.int32, (ts, ts), 0)
    ci = lax.broadcasted_iota(jnp.int32, (ts, ts), 1)
    same = (ri // GDN_CHUNK) == (ci // GDN_CHUNK)
    low = jnp.where(same & (ci <= ri), 1.0, 0.0).astype(F32)
    upp = jnp.where(same & (ri <= ci), 1.0, 0.0).astype(F32)
    gc_c = _hdot(low, g_c)
    gc_r = _hdot(g_r, upp)
    bcol_ref[0] = beta_c
    gcol_ref[0] = gc_c
    rs_ref[0, 0:nh, :] = beta_r
    rs_ref[0, nh:2 * nh, :] = gc_r
    nchunk = ts // GDN_CHUNK
    cr = lax.broadcasted_iota(jnp.int32, (nchunk, ts), 0)
    ct = lax.broadcasted_iota(jnp.int32, (nchunk, ts), 1)
    chunk_sum = jnp.where(ct // GDN_CHUNK == cr, 1.0, 0.0).astype(F32)
    gl = _hdot(chunk_sum, g_c)
    er = lax.broadcasted_iota(jnp.int32, (nh, GDN_NV), 0)
    ec = lax.broadcasted_iota(jnp.int32, (nh, GDN_NV), 1)
    expand = jnp.where(ec // GDN_DV == er, 1.0, 0.0).astype(F32)
    egl_ref[0] = jnp.exp(_hdot(gl, expand))


def _split_bf16(v):
    hi = v.astype(BF16)
    return hi, (v - hi.astype(F32)).astype(BF16)


def _dot_x3(ah, al, bh, bl):
    return _dot(ah, bh) + (_dot(al, bh) + _dot(ah, bl))


def _unit_lower_inverses(mats, eye, order):
    n = mats[0].shape[0]
    ts = [eye - a for a in mats]
    ps = [_dot_x3(ah, al, ah, al) for ah, al in map(_split_bf16, mats)]
    for _ in range(order.bit_length() - 3):
        tsp, psp = list(map(_split_bf16, ts)), list(map(_split_bf16, ps))
        tps = [_dot_x3(jnp.concatenate([th, ph], axis=0), jnp.concatenate([tl, pl_], axis=0), ph, pl_)
               for (th, tl), (ph, pl_) in zip(tsp, psp)]
        ts = [t + tp[:n] for t, tp in zip(ts, tps)]
        ps = [tp[n:] for tp in tps]
    tsp, psp = list(map(_split_bf16, ts)), list(map(_split_bf16, ps))
    return [t + _dot_x3(th, tl, ph, pl_) for t, (th, tl), (ph, pl_) in zip(ts, tsp, psp)]


def _gdn_pre_kernel(q_ref, k_ref, v_ref, bcol_ref, gcol_ref, rs_ref, w_ref, u_ref, qh_ref, kt_ref, in_ref):
    tc, c, nh = GDN_TC, GDN_CHUNK, GDN_V_HEADS
    j = pl.program_id(1)
    head_lane = lax.broadcasted_iota(jnp.int32, (tc, nh), 1)
    ri = lax.broadcasted_iota(jnp.int32, (tc, tc), 0)
    ci = lax.broadcasted_iota(jnp.int32, (tc, tc), 1)
    same = (ri // c) == (ci // c)
    tril = same & (ri >= ci)
    strict = same & (ri > ci)
    eye = jnp.where(ri == ci, 1.0, 0.0).astype(F32)
    lane = lax.broadcasted_iota(jnp.int32, (1, LANES), 1)

    chains = []
    for g in range(GDN_TP // tc):
        rows = slice(g * tc, (g + 1) * tc)
        q, k = q_ref[0, rows, :], k_ref[0, rows, :]
        qb, kb = q.astype(BF16), k.astype(BF16)
        kk = lax.dot_general(kb, kb, NT_DIMS, preferred_element_type=F32)
        qk = lax.dot_general(qb, kb, NT_DIMS, preferred_element_type=F32)
        for hh in range(2):
            h = 2 * j + hh
            sel = head_lane == h
            beta_c = jnp.sum(jnp.where(sel, bcol_ref[0, rows, :], 0.0), -1, keepdims=True)
            gc_c = jnp.sum(jnp.where(sel, gcol_ref[0, rows, :], 0.0), -1, keepdims=True)
            gl_c = jnp.concatenate([jnp.broadcast_to(gc_c[(m + 1) * c - 1:(m + 1) * c, :], (c, 1))
                                    for m in range(tc // c)], axis=0)
            beta_r = rs_ref[0, pl.ds(h, 1), rows]
            gc_r = rs_ref[0, pl.ds(nh + h, 1), rows]
            decay = jnp.exp(jnp.where(tril, gc_c - gc_r, -jnp.inf))
            chains.append(dict(a=jnp.where(strict, beta_c * kk * decay, 0.0), rows=rows, hh=hh, q=q, k=k, kb=kb,
                               qk=qk, decay=decay, beta_r=beta_r, gc_r=gc_r, gc_c=gc_c, gl_c=gl_c))
    inverses = _unit_lower_inverses([ch["a"] for ch in chains], eye, c)

    intra_prev = None
    for ch, tm in zip(chains, inverses):
        rows, hh = ch["rows"], ch["hh"]
        cols = slice(hh * GDN_DV, (hh + 1) * GDN_DV)
        vh = v_ref[0, rows, cols].astype(BF16)
        u_ref[0, rows, cols] = _dot((tm * ch["beta_r"]).astype(BF16), vh)
        w_ref[0, rows, cols] = _dot((tm * (ch["beta_r"] * jnp.exp(ch["gc_r"]))).astype(BF16), ch["kb"]).astype(BF16)
        qh_ref[0, rows, cols] = (ch["q"] * jnp.exp(ch["gc_c"])).astype(BF16)
        kt_ref[0, rows, cols] = (ch["k"] * jnp.exp(ch["gl_c"] - ch["gc_c"])).astype(BF16)
        intra = jnp.where(tril, ch["qk"] * ch["decay"], 0.0)
        half = intra[:, :LANES] + intra[:, LANES:]
        full = half + pltpu.roll(half, c, axis=1)
        if hh == 1:
            in_ref[0, rows, :] = jnp.where(lane < c, intra_prev, full).astype(BF16)
        intra_prev = full


def _gdn_scan_kernel(w_ref, u_ref, qh_ref, kt_ref, in_ref, egl_ref, o_ref, s_ref):
    c = GDN_CHUNK
    pw = 2 * GDN_DV
    ri = lax.broadcasted_iota(jnp.int32, (pw, pw), 0)
    ci = lax.broadcasted_iota(jnp.int32, (pw, pw), 1)
    block_diag = (ri // GDN_DK) == (ci // GDN_DV)

    @pl.when(pl.program_id(1) == 0)
    def _():
        s_ref[...] = jnp.zeros(s_ref.shape, F32)

    def step(n, carry):
        r0 = pl.multiple_of(n * c, c)
        rows = pl.ds(r0, c)
        for p in range(GDN_K_HEADS):
            cols = slice(p * pw, (p + 1) * pw)
            s = s_ref[p]
            wq = jnp.concatenate([w_ref[0, rows, cols], qh_ref[0, rows, cols]], axis=0)
            r = _dot(wq, s.astype(BF16))
            vn = (u_ref[0, rows, cols] - r[:c]).astype(BF16)
            zero = jnp.zeros((c, GDN_DV), BF16)
            vbd = jnp.concatenate([jnp.concatenate([vn[:, :GDN_DV], zero], axis=1),
                                   jnp.concatenate([zero, vn[:, GDN_DV:]], axis=1)], axis=0)
            o_ref[0, rows, cols] = r[c:] + _dot(in_ref[0, rows, p * 2 * c:(p + 1) * 2 * c], vbd)
            upd = lax.dot_general(kt_ref[0, rows, cols], vn, TN_DIMS, preferred_element_type=F32)
            s_ref[p] = s * egl_ref[0, pl.ds(n, 1), cols] + jnp.where(block_diag, upd, 0.0)
        return carry

    lax.fori_loop(0, GDN_TS // c, step, 0)


def _gdn_out_kernel(o_ref, z_ref, x_ref, ng_ref, wout_ref, g_ref, b_ref, y_ref):
    parts = []
    for h in range(GDN_V_HEADS):
        cols = slice(h * GDN_DV, (h + 1) * GDN_DV)
        oh = o_ref[0, :, cols]
        oh = oh * lax.rsqrt(jnp.mean(oh * oh, -1, keepdims=True) + RMS_EPS) * ng_ref[...]
        parts.append((oh * _silu(z_ref[0, :, cols].astype(F32))).astype(BF16))
    mix = _dot(jnp.concatenate(parts, axis=1), wout_ref[...])
    y_ref[0] = _layer_norm(DN_ALPHA * x_ref[0] + mix, g_ref[...], b_ref[...])


def _gdn_ln(x, w_in, w_conv, a_log, dt_bias, norm_g, w_out, g, b):
    bn, s, d = x.shape
    nh, ts = GDN_V_HEADS, GDN_TS
    nchunks = s // GDN_CHUNK
    w_qkv = w_in[:, :GDN_QKV].astype(BF16)
    w_z = w_in[:, GDN_QKV:GDN_QKV + GDN_NV].astype(BF16)
    w_ba = w_in[:, GDN_QKV + GDN_NV:].astype(BF16)
    tile = lambda w: pl.BlockSpec((1, ts, w), lambda i, jj: (i, jj, 0))
    q, k, v, z, bcol, gcol, rs, egl = pl.pallas_call(
        _gdn_in_kernel,
        grid=(bn, s // ts),
        in_specs=[tile(d), _const_spec(w_qkv.shape), _const_spec(w_conv.shape), _const_spec(w_z.shape),
                  _const_spec(w_ba.shape), _const_spec((2 * nh, d)), _const_spec((1, nh)), _const_spec((1, nh)),
                  _const_spec((nh, 1)), _const_spec((nh, 1))],
        out_specs=[tile(GDN_NK), tile(GDN_NK), tile(GDN_NV), tile(GDN_NV), tile(nh), tile(nh),
                   pl.BlockSpec((1, 2 * nh, ts), lambda i, jj: (i, 0, jj)),
                   pl.BlockSpec((1, ts // GDN_CHUNK, GDN_NV), lambda i, jj: (i, jj, 0))],
        out_shape=[jax.ShapeDtypeStruct((bn, s, GDN_NK), F32), jax.ShapeDtypeStruct((bn, s, GDN_NK), F32),
                   jax.ShapeDtypeStruct((bn, s, GDN_NV), F32), jax.ShapeDtypeStruct((bn, s, GDN_NV), BF16),
                   jax.ShapeDtypeStruct((bn, s, nh), F32), jax.ShapeDtypeStruct((bn, s, nh), F32),
                   jax.ShapeDtypeStruct((bn, 2 * nh, s), F32),
                   jax.ShapeDtypeStruct((bn, nchunks, GDN_NV), F32)],
        scratch_shapes=[pltpu.VMEM((GDN_HALO + ts, GDN_QKV), F32)],
        compiler_params=_cparams("parallel", "arbitrary"),
        name="gdn_in",
    )(x, w_qkv, w_conv, w_z, w_ba, w_ba.T, a_log.reshape(1, nh), dt_bias.reshape(1, nh),
      a_log.reshape(nh, 1), dt_bias.reshape(nh, 1))

    tc = GDN_TP
    pair = lambda w: pl.BlockSpec((1, tc, w), lambda i, jj, tt: (i, tt, jj))
    whole = lambda w: pl.BlockSpec((1, tc, w), lambda i, jj, tt: (i, tt, 0))
    w, u, qh, kt, intra = pl.pallas_call(
        _gdn_pre_kernel,
        grid=(bn, GDN_K_HEADS, s // tc),
        in_specs=[pair(GDN_DK), pair(GDN_DK), pair(2 * GDN_DV), whole(nh), whole(nh),
                  pl.BlockSpec((1, 2 * nh, tc), lambda i, jj, tt: (i, 0, tt))],
        out_specs=[pair(2 * GDN_DK), pair(2 * GDN_DV), pair(2 * GDN_DK), pair(2 * GDN_DK), pair(2 * GDN_CHUNK)],
        out_shape=[jax.ShapeDtypeStruct((bn, s, 2 * GDN_NK), BF16), jax.ShapeDtypeStruct((bn, s, GDN_NV), F32),
                   jax.ShapeDtypeStruct((bn, s, 2 * GDN_NK), BF16), jax.ShapeDtypeStruct((bn, s, 2 * GDN_NK), BF16),
                   jax.ShapeDtypeStruct((bn, s, GDN_V_HEADS * GDN_CHUNK), BF16)],
        compiler_params=_cparams("parallel", "parallel", "parallel"),
        name="gdn_pre",
    )(q, k, v, bcol, gcol, rs)

    o = pl.pallas_call(
        _gdn_scan_kernel,
        grid=(bn, s // ts),
        in_specs=[tile(2 * GDN_NK), tile(GDN_NV), tile(2 * GDN_NK), tile(2 * GDN_NK),
                  tile(GDN_V_HEADS * GDN_CHUNK),
                  pl.BlockSpec((1, ts // GDN_CHUNK, GDN_NV), lambda i, jj: (i, jj, 0))],
        out_specs=tile(GDN_NV),
        out_shape=jax.ShapeDtypeStruct((bn, s, GDN_NV), F32),
        scratch_shapes=[pltpu.VMEM((GDN_K_HEADS, 2 * GDN_DK, 2 * GDN_DV), F32)],
        compiler_params=_cparams("parallel", "arbitrary"),
        name="gdn_scan",
    )(w, u, qh, kt, intra, egl)

    return pl.pallas_call(
        _gdn_out_kernel,
        grid=(bn, s // ts),
        in_specs=[tile(GDN_NV), tile(GDN_NV), tile(d), _const_spec((1, GDN_DV)), _const_spec(w_out.shape),
                  _const_spec((1, d)), _const_spec((1, d))],
        out_specs=tile(d),
        out_shape=jax.ShapeDtypeStruct(x.shape, F32),
        compiler_params=_cparams("parallel", "parallel"),
        name="gdn_out",
    )(o, z, x, norm_g.reshape(1, GDN_DV), w_out.astype(BF16), g.reshape(1, d), b.reshape(1, d))


ATT_HEADS = D_MODEL // 64
ATT_HEAD_DIM = 64
ROPE_DIM = ATT_HEAD_DIM // 4
ROPE_HALF = ROPE_DIM // 2
ATT_NOPE_DIM = ATT_HEAD_DIM - ROPE_DIM
Q_RANK = D_MODEL // 4
KV_RANK = D_MODEL // 8
IDX_HEADS = 8
IDX_DIM = ATT_HEAD_DIM
TOPK_MAX = 256
QK_WIDTH = 2 * LANES
DSA_TS = 512
DSA_QB = 128
DSA_KT = 512
MASKED = -1e30


def _rope_tab_kernel(pos_ref, inv_ref, rot_ref, sgn_ref, cos_ref, sin_ref):
    ang = pos_ref[0].astype(F32) * inv_ref[...]
    cos_ref[0] = jnp.where(rot_ref[...] > 0.0, jnp.cos(ang), 1.0)
    sin_ref[0] = jnp.sin(ang) * sgn_ref[...]


def _rope_tables(positions):
    bn, s = positions.shape
    lane = jnp.arange(LANES)
    inv = ROPE_THETA ** (-jnp.arange(0, ROPE_DIM, 2, dtype=F32) / ROPE_DIM)
    within = lane % ATT_HEAD_DIM
    rot = (within < ROPE_DIM).astype(F32)
    sgn = jnp.where(within < ROPE_HALF, -1.0, 1.0).astype(F32) * rot
    row = lambda v: v.reshape(1, LANES)
    ts = DSA_TS
    tile = pl.BlockSpec((1, ts, LANES), lambda i, j: (i, j, 0))
    return pl.pallas_call(
        _rope_tab_kernel,
        grid=(bn, s // ts),
        in_specs=[pl.BlockSpec((1, ts, 1), lambda i, j: (i, j, 0))] + [_const_spec((1, LANES))] * 3,
        out_specs=[tile, tile],
        out_shape=[jax.ShapeDtypeStruct((bn, s, LANES), F32)] * 2,
        compiler_params=_cparams("parallel", "parallel"),
        name="rope_tables",
    )(positions.reshape(bn, s, 1), row(inv[lane % ROPE_HALF]), row(rot), row(sgn))


def _rope(v, cosm, sinm):
    lane = lax.broadcasted_iota(jnp.int32, (1, LANES), 1)
    low = (lane % ROPE_DIM) < ROPE_HALF
    cols = []
    for c0 in range(0, v.shape[1], LANES):
        blk = v[:, c0:c0 + LANES]
        partner = jnp.where(low, pltpu.roll(blk, LANES - ROPE_HALF, axis=1), pltpu.roll(blk, ROPE_HALF, axis=1))
        cols.append(blk * cosm + partner * sinm)
    return cols[0] if len(cols) == 1 else jnp.concatenate(cols, axis=1)


def _rms_norm(v, g):
    return v * lax.rsqrt(jnp.mean(v * v, -1, keepdims=True) + RMS_EPS) * g


def _dsa_proj_kernel(x_ref, cos_ref, sin_ref, wdq_ref, qn_ref, wuq_ref, wcomb_ref, wdkv_ref, kvn_ref, wkr_ref,
                     wiq_ref, wik_ref, ikg_ref, ikb_ref, prot_ref, wiwt_ref,
                     q2_ref, k2_ref, qi_ref, ki_ref, wit_ref):
    xb = x_ref[0].astype(BF16)
    cosm, sinm = cos_ref[0], sin_ref[0]
    cq = _rms_norm(_dot(xb, wdq_ref[...]), qn_ref[...]).astype(BF16)
    q = _rope(_dot(cq, wuq_ref[...]), cosm, sinm) * ATT_HEAD_DIM ** -0.5
    q2_ref[0] = _dot(q.astype(BF16), wcomb_ref[...]).astype(BF16)
    ckv = _rms_norm(_dot(xb, wdkv_ref[...]), kvn_ref[...])
    kr = _rope(_dot(xb, wkr_ref[...]), cosm, sinm)
    k2_ref[0] = jnp.concatenate([ckv, kr], axis=1).astype(BF16)
    qi = _rope(_dot(cq, wiq_ref[...]), cosm, sinm).astype(BF16)
    for h in range(IDX_HEADS):
        qi_ref[0, h] = qi[:, h * IDX_DIM:(h + 1) * IDX_DIM]
    kin = _layer_norm(_dot(xb, wik_ref[...]), ikg_ref[...], ikb_ref[...])
    ki = kin * cosm[:, :IDX_DIM] + _hdot(kin, prot_ref[...]) * sinm[:, :IDX_DIM]
    ki_ref[0] = ki.astype(BF16)
    wit = lax.dot_general(wiwt_ref[...], xb, NT_DIMS, preferred_element_type=F32)
    wit_ref[0] = wit * (IDX_HEADS * IDX_DIM) ** -0.5


def _dsa_attn_kernel(q2_ref, k2_ref, qi_ref, ki_ref, wit_ref, o_ref,
                     key_ref, mask_ref, q2s_ref, lga_ref, lgb_ref, pa_ref, pb_ref, acc_ref):
    qb, kt_sz = DSA_QB, DSA_KT
    i = pl.program_id(1)
    n_tiles = (i * qb) // kt_sz + 1
    q_pos = i * qb + lax.broadcasted_iota(jnp.int32, (1, qb), 1)

    def score_tile(t, carry):
        r0 = pl.multiple_of(t * kt_sz, kt_sz)
        kblk = ki_ref[0, pl.ds(r0, kt_sz), :]
        acc = jnp.zeros((kt_sz, qb), F32)
        for h in range(IDX_HEADS):
            sc = lax.dot_general(kblk, qi_ref[0, h], NT_DIMS, preferred_element_type=F32)
            acc = acc + jnp.maximum(sc, 0.0) * wit_ref[0, h:h + 1, :]
        k_pos = r0 + lax.broadcasted_iota(jnp.int32, (kt_sz, 1), 0)
        acc = jnp.where(k_pos <= q_pos, acc, -jnp.inf)
        bits = pltpu.bitcast(acc, jnp.int32)
        key_ref[pl.ds(r0, kt_sz), :] = bits ^ ((bits >> 31) & jnp.int32(0x7FFFFFFF))
        return carry

    lax.fori_loop(0, n_tiles, score_tile, 0)

    def count(pred):
        def body(t, acc):
            r0 = pl.multiple_of(t * kt_sz, kt_sz)
            k_pos = r0 + lax.broadcasted_iota(jnp.int32, (kt_sz, 1), 0)
            hit = jnp.where(pred(key_ref[pl.ds(r0, kt_sz), :], k_pos), 1, 0).astype(jnp.int32)
            return acc + jnp.sum(hit.reshape(kt_sz // SUBLANES, SUBLANES, qb), axis=0)
        acc = lax.fori_loop(0, n_tiles, body, jnp.zeros((SUBLANES, qb), jnp.int32))
        return jnp.sum(acc, axis=0, keepdims=True)

    int_min = jnp.int32(-2 ** 31)
    topk = jnp.int32(TOPK_MAX)

    def search_bit(b, thr):
        cand = thr + lax.shift_left(jnp.int32(1), jnp.int32(31) - b)
        return jnp.where(count(lambda kb, kp: kb >= cand) >= topk, cand, thr)

    few_keys = (i + 1) * qb <= TOPK_MAX
    thr0 = jnp.full((1, qb), int_min, jnp.int32)
    thr = lax.cond(few_keys, lambda: thr0, lambda: lax.fori_loop(0, 32, search_bit, thr0))

    need = topk - count(lambda kb, kp: kb > thr)
    n_eq = count(lambda kb, kp: kb == thr)
    full_cut = jnp.full((1, qb), jnp.int32(2 ** 30), jnp.int32)

    def search_cut():
        def cut_bit(b, cut):
            cand = cut + lax.shift_left(jnp.int32(1), jnp.int32(12) - b)
            below = count(lambda kb, kp: (kb == thr) & (kp < cand))
            return jnp.where(below < need, cand, cut)
        return lax.fori_loop(0, 13, cut_bit, jnp.zeros((1, qb), jnp.int32))

    tied = jnp.logical_and(jnp.logical_not(few_keys), jnp.max(jnp.where(n_eq != need, 1, 0)) > 0)
    cut = lax.cond(tied, search_cut, lambda: full_cut)

    def mask_tile(t, carry):
        r0 = pl.multiple_of(t * kt_sz, kt_sz)
        kb = key_ref[pl.ds(r0, kt_sz), :]
        k_pos = r0 + lax.broadcasted_iota(jnp.int32, (kt_sz, 1), 0)
        keep = ((kb > thr) | ((kb == thr) & (k_pos <= cut))) & (k_pos <= q_pos)
        mask_ref[:, pl.ds(r0, kt_sz)] = jnp.where(keep, 0.0, MASKED).astype(F32).T
        return carry

    lax.fori_loop(0, n_tiles, mask_tile, 0)

    nh = ATT_HEADS
    half = kt_sz // 2
    for h in range(nh):
        q2s_ref[h * qb:(h + 1) * qb, :] = q2_ref[0, :, h * QK_WIDTH:(h + 1) * QK_WIDTH]
    acc_ref[...] = jnp.zeros(acc_ref.shape, F32)
    vlane = lax.broadcasted_iota(jnp.int32, (1, QK_WIDTH), 1)
    ones_col = jnp.where(vlane == KV_RANK, 1.0, 0.0).astype(BF16)

    def logits_into(dst_ref, key0):
        dst_ref[...] = lax.dot_general(q2s_ref[...], k2_ref[0, pl.ds(pl.multiple_of(key0, half), half), :],
                                       NT_DIMS, preferred_element_type=F32)

    def softmax_pv(src_ref, prob_ref, key0, m):
        key0 = pl.multiple_of(key0, half)
        mask = mask_ref[:, pl.ds(key0, half)]
        wide = lambda v: jnp.concatenate([v] * (half // LANES), axis=1)
        m_parts, alpha_parts = [], []
        for h in range(nh):
            rows = slice(h * qb, (h + 1) * qb)
            lg = src_ref[rows, :] + mask
            m_new = jnp.maximum(m[rows], jnp.max(lg, -1, keepdims=True))
            prob_ref[rows, :] = jnp.exp(lg - wide(m_new)).astype(BF16)
            alpha_parts.append(jnp.exp(m[rows] - m_new))
            m_parts.append(m_new)
        v1 = jnp.where(vlane < KV_RANK, k2_ref[0, pl.ds(key0, half), :], ones_col)
        alpha = jnp.concatenate(alpha_parts, axis=0)
        acc_ref[...] = acc_ref[...] * jnp.concatenate([alpha] * (QK_WIDTH // LANES), axis=1) + _dot(prob_ref[...], v1)
        return jnp.concatenate(m_parts, axis=0)

    logits_into(lga_ref, 0)
    last_key0 = (n_tiles - 1) * kt_sz

    def attn_tile(t, m):
        key0 = t * kt_sz
        logits_into(lgb_ref, key0 + half)
        m = softmax_pv(lga_ref, pa_ref, key0, m)
        logits_into(lga_ref, jnp.minimum(key0 + kt_sz, last_key0))
        return softmax_pv(lgb_ref, pb_ref, key0 + half, m)

    lax.fori_loop(0, n_tiles, attn_tile, jnp.full((nh * qb, LANES), MASKED, F32))
    for h in range(nh):
        acc = acc_ref[h * qb:(h + 1) * qb, :]
        o_ref[0, :, h * KV_RANK:(h + 1) * KV_RANK] = (
            acc[:, :KV_RANK] / acc[:, KV_RANK:KV_RANK + 1]).astype(o_ref.dtype)


def _dsa_out_kernel(o_ref, x_ref, wuv_ref, wo_ref, g_ref, b_ref, y_ref):
    o = _dot(o_ref[0], wuv_ref[...]).astype(BF16)
    mix = _dot(o, wo_ref[...])
    y_ref[0] = _layer_norm(DN_ALPHA * x_ref[0] + mix, g_ref[...], b_ref[...])


def _block_diag(blocks):
    n, r, c = blocks.shape
    idx = jnp.arange(n)
    return jnp.zeros((n, r, n, c), blocks.dtype).at[idx, :, idx, :].set(blocks).reshape(n * r, n * c)


def _dsa_ln(x, positions, w_dq, q_norm, w_uq, w_dkv, kv_norm, w_kr, w_uk, w_uv, w_o,
            w_iq, w_ik, ik_g, ik_b, w_iw, g, b):
    bn, s, d = x.shape
    ts, nh = DSA_TS, ATT_HEADS
    cosm, sinm = _rope_tables(positions)

    head_map = jnp.zeros((nh, ATT_HEAD_DIM, QK_WIDTH), F32)
    head_map = head_map.at[:, ROPE_DIM:, :KV_RANK].set(w_uk)
    head_map = head_map.at[:, :ROPE_DIM, KV_RANK:KV_RANK + ROPE_DIM].set(jnp.eye(ROPE_DIM, dtype=F32))
    w_comb = _block_diag(head_map).astype(BF16)
    w_kr_pad = jnp.zeros((d, LANES), F32).at[:, :ROPE_DIM].set(w_kr).astype(BF16)
    lane = jnp.arange(IDX_DIM)
    partner = jnp.where(lane < ROPE_HALF, lane + ROPE_HALF, lane - ROPE_HALF)
    p_rot = ((lane[:, None] == partner[None, :]) & (lane[None, :] < ROPE_DIM)).astype(F32)

    tile = lambda w: pl.BlockSpec((1, ts, w), lambda i, j: (i, j, 0))
    consts = [w_dq.astype(BF16), q_norm.reshape(1, -1), w_uq.astype(BF16), w_comb, w_dkv.astype(BF16),
              kv_norm.reshape(1, -1), w_kr_pad, w_iq.astype(BF16), w_ik.astype(BF16), ik_g.reshape(1, -1),
              ik_b.reshape(1, -1), p_rot, w_iw.T.astype(BF16)]
    q2, k2, qi, ki, wit = pl.pallas_call(
        _dsa_proj_kernel,
        grid=(bn, s // ts),
        in_specs=[tile(d), tile(LANES), tile(LANES)] + [_const_spec(c.shape) for c in consts],
        out_specs=[tile(nh * QK_WIDTH), tile(QK_WIDTH),
                   pl.BlockSpec((1, IDX_HEADS, ts, IDX_DIM), lambda i, j: (i, 0, j, 0)),
                   tile(IDX_DIM), pl.BlockSpec((1, IDX_HEADS, ts), lambda i, j: (i, 0, j))],
        out_shape=[jax.ShapeDtypeStruct((bn, s, nh * QK_WIDTH), BF16), jax.ShapeDtypeStruct((bn, s, QK_WIDTH), BF16),
                   jax.ShapeDtypeStruct((bn, IDX_HEADS, s, IDX_DIM), BF16),
                   jax.ShapeDtypeStruct((bn, s, IDX_DIM), BF16), jax.ShapeDtypeStruct((bn, IDX_HEADS, s), F32)],
        compiler_params=_cparams("parallel", "parallel"),
        name="dsa_proj",
    )(x, cosm, sinm, *consts)

    qb = DSA_QB
    o_lat = pl.pallas_call(
        _dsa_attn_kernel,
        grid=(bn, s // qb),
        in_specs=[pl.BlockSpec((1, qb, nh * QK_WIDTH), lambda i, j: (i, j, 0)),
                  pl.BlockSpec((1, s, QK_WIDTH), lambda i, j: (i, 0, 0)),
                  pl.BlockSpec((1, IDX_HEADS, qb, IDX_DIM), lambda i, j: (i, 0, j, 0)),
                  pl.BlockSpec((1, s, IDX_DIM), lambda i, j: (i, 0, 0)),
                  pl.BlockSpec((1, IDX_HEADS, qb), lambda i, j: (i, 0, j))],
        out_specs=pl.BlockSpec((1, qb, nh * KV_RANK), lambda i, j: (i, j, 0)),
        out_shape=jax.ShapeDtypeStruct((bn, s, nh * KV_RANK), BF16),
        scratch_shapes=[pltpu.VMEM((s, qb), jnp.int32), pltpu.VMEM((qb, s), F32),
                        pltpu.VMEM((nh * qb, QK_WIDTH), BF16),
                        pltpu.VMEM((nh * qb, DSA_KT // 2), F32), pltpu.VMEM((nh * qb, DSA_KT // 2), F32),
                        pltpu.VMEM((nh * qb, DSA_KT // 2), BF16), pltpu.VMEM((nh * qb, DSA_KT // 2), BF16),
                        pltpu.VMEM((nh * qb, QK_WIDTH), F32)],
        compiler_params=_cparams("parallel", "parallel"),
        name="dsa_attn",
    )(q2, k2, qi, ki, wit)

    w_uv_bd = _block_diag(w_uv).astype(BF16)
    return pl.pallas_call(
        _dsa_out_kernel,
        grid=(bn, s // ts),
        in_specs=[tile(nh * KV_RANK), tile(d), _const_spec(w_uv_bd.shape), _const_spec(w_o.shape),
                  _const_spec((1, d)), _const_spec((1, d))],
        out_specs=tile(d),
        out_shape=jax.ShapeDtypeStruct(x.shape, F32),
        compiler_params=_cparams("parallel", "parallel"),
        name="dsa_out",
    )(o_lat, x, w_uv_bd, w_o.astype(BF16), g.reshape(1, d), b.reshape(1, d))


def kernel(x, positions, ln_g, ln_b, ffn_w_gate, ffn_w_up, ffn_w_down,
           conv_w_in, conv_w_dw, conv_ln_g, conv_ln_b, conv_w_out,
           sc_w_in, sc_w_conv, sc_w_out,
           dsa_w_dq, dsa_q_norm, dsa_w_uq, dsa_w_dkv, dsa_kv_norm, dsa_w_kr,
           dsa_w_uk, dsa_w_uv, dsa_w_o, dsa_w_iq, dsa_w_ik, dsa_ik_ln_g, dsa_ik_ln_b, dsa_w_iw,
           gdn_w_in, gdn_w_conv, gdn_a_log, gdn_dt_bias, gdn_norm_g, gdn_w_out):
    bn, s, d = x.shape

    def ffn(h, i, half):
        return _ffn_ln(h.reshape(bn * s, d), ffn_w_gate[i, half], ffn_w_up[i, half], ffn_w_down[i, half],
                       ln_g[i, 2 * half], ln_b[i, 2 * half]).reshape(bn, s, d)

    for i in range(DEPTH):
        m, j = i % N_MIXERS, i // N_MIXERS
        x = ffn(x, i, 0)
        g, b = ln_g[i, 1], ln_b[i, 1]
        if m == 0:
            x = _conformer_ln(x, conv_w_in[j], conv_w_dw[j], conv_ln_g[j], conv_ln_b[j], conv_w_out[j], g, b)
        elif m == 1:
            x = _sconv_ln(x, sc_w_in[j], sc_w_conv[j], sc_w_out[j], g, b)
        elif m == 2:
            x = _dsa_ln(x, positions, dsa_w_dq[j], dsa_q_norm[j], dsa_w_uq[j], dsa_w_dkv[j], dsa_kv_norm[j],
                        dsa_w_kr[j], dsa_w_uk[j], dsa_w_uv[j], dsa_w_o[j], dsa_w_iq[j], dsa_w_ik[j],
                        dsa_ik_ln_g[j], dsa_ik_ln_b[j], dsa_w_iw[j], g, b)
        else:
            x = _gdn_ln(x, gdn_w_in[j], gdn_w_conv[j], gdn_a_log[j], gdn_dt_bias[j], gdn_norm_g[j],
                        gdn_w_out[j], g, b)
        x = ffn(x, i, 1)
    return x
```

```python
import functools

import jax
import jax.numpy as jnp
from jax import lax
from jax.experimental import pallas as pl
from jax.experimental.pallas import tpu as pltpu

D_MODEL = 1024
DEPTH = 4
N_MIXERS = 4
FFN_DIM = ((8 * D_MODEL // 3 + 127) // 128) * 128
DN_ALPHA = (2.0 * DEPTH) ** 0.25
ROPE_THETA = 500000.0
LN_EPS = 1e-5
RMS_EPS = 1e-6
CONF_KERNEL = 31
SCONV_KERNEL = 3

BF16 = jnp.bfloat16
F32 = jnp.float32

V7X_VMEM_LIMIT_BYTES = 56 * 1024 * 1024
SUBLANES = 8
LANES = 128


def _cparams(*sem):
    return pltpu.CompilerParams(dimension_semantics=sem, vmem_limit_bytes=V7X_VMEM_LIMIT_BYTES)


def _layer_norm(y, g, b):
    mu = jnp.mean(y, -1, keepdims=True)
    yc = y - mu
    var = jnp.mean(yc * yc, -1, keepdims=True)
    return yc * lax.rsqrt(var + LN_EPS) * g + b


def _silu(h):
    return h * jax.nn.sigmoid(h)


def _dot(a, b):
    return jnp.dot(a, b, preferred_element_type=F32)


def _const_spec(shape):
    nd = len(shape)
    return pl.BlockSpec(shape, lambda *_: (0,) * nd, pipeline_mode=pl.Buffered(1))


FFN_TM = 512
FFN_CHUNK = 768


def _ffn_kernel(x_ref, wg_ref, wu_ref, wd_ref, g_ref, b_ref, o_ref):
    x = x_ref[...]
    xb = x.astype(BF16)
    bounds = list(range(0, FFN_DIM, FFN_CHUNK)) + [FFN_DIM]
    gate_up = lambda c: (_dot(xb, wg_ref[:, bounds[c]:bounds[c + 1]]), _dot(xb, wu_ref[:, bounds[c]:bounds[c + 1]]))
    acc = jnp.zeros(x.shape, F32)
    h, u = gate_up(0)
    for c in range(len(bounds) - 1):
        nxt = gate_up(c + 1) if c + 2 < len(bounds) else None
        a = (_silu(h) * u).astype(BF16)
        acc = acc + _dot(a, wd_ref[bounds[c]:bounds[c + 1], :])
        if nxt is not None:
            h, u = nxt
    o_ref[...] = _layer_norm(DN_ALPHA * x + 0.5 * acc, g_ref[...], b_ref[...])


def _ffn_ln(x2, wg, wu, wd, g, b):
    t, d = x2.shape
    row = pl.BlockSpec((FFN_TM, d), lambda i: (i, 0))
    return pl.pallas_call(
        _ffn_kernel,
        grid=(t // FFN_TM,),
        in_specs=[row, _const_spec(wg.shape), _const_spec(wu.shape), _const_spec(wd.shape),
                  _const_spec((1, d)), _const_spec((1, d))],
        out_specs=row,
        out_shape=jax.ShapeDtypeStruct((t, d), F32),
        compiler_params=_cparams("parallel"),
        name="ffn_ln",
    )(x2, wg.astype(BF16), wu.astype(BF16), wd.astype(BF16), g.reshape(1, d), b.reshape(1, d))


def _causal_dwconv(buf_ref, u, w_ref, width, halo, ts, first_tile):
    c = u.shape[-1]

    @pl.when(first_tile)
    def _():
        buf_ref[0:halo, :] = jnp.zeros((halo, c), F32)

    @pl.when(jnp.logical_not(first_tile))
    def _():
        buf_ref[0:halo, :] = buf_ref[ts:ts + halo, :]

    buf_ref[halo:halo + ts, :] = u
    off = halo - (width - 1)
    acc = w_ref[width - 1:width, :] * u
    for k in range(width - 1):
        acc = acc + w_ref[k:k + 1, :] * buf_ref[off + k:off + k + ts, :]
    return acc


def _causal_dwconv_wide(buf_ref, sh_ref, out_ref, u, w_ref, width, halo, ts, first_tile):
    c = u.shape[-1]

    @pl.when(first_tile)
    def _():
        buf_ref[0:halo, :] = jnp.zeros((halo, c), F32)

    @pl.when(jnp.logical_not(first_tile))
    def _():
        buf_ref[0:halo, :] = buf_ref[ts:ts + halo, :]

    buf_ref[halo:halo + ts, :] = u
    off = halo - (width - 1)
    n = halo + ts - SUBLANES
    for b in range(1, SUBLANES):
        sh_ref[b - 1, 0:n, :] = buf_ref[b:b + n, :]

    def block(i, carry):
        r = pl.multiple_of(i * DWCONV_ROWS, DWCONV_ROWS)
        acc = w_ref[width - 1:width, :] * buf_ref[pl.ds(halo + r, DWCONV_ROWS), :]
        for k in range(width - 1):
            a, b = divmod(off + k, SUBLANES)
            src = buf_ref if b == 0 else sh_ref.at[b - 1]
            acc = acc + w_ref[k:k + 1, :] * src[pl.ds(a * SUBLANES + r, DWCONV_ROWS), :]
        out_ref[pl.ds(r, DWCONV_ROWS), :] = acc
        return carry

    lax.fori_loop(0, ts // DWCONV_ROWS, block, 0)
    return out_ref[...]


CONV_TS = 512
CONF_HALO = 32
DWCONV_ROWS = 16


def _conf_kernel(x_ref, win_ref, wdw_ref, lng_ref, lnb_ref, wout_ref, g_ref, b_ref, o_ref, buf_ref, sh_ref, cv_ref):
    d = D_MODEL
    x = x_ref[0]
    uu = _dot(x.astype(BF16), win_ref[...])
    u = uu[:, :d] * jax.nn.sigmoid(uu[:, d:])
    cv = _causal_dwconv_wide(buf_ref, sh_ref, cv_ref, u, wdw_ref, CONF_KERNEL, CONF_HALO, CONV_TS,
                             pl.program_id(1) == 0)
    v = _silu(_layer_norm(cv, lng_ref[...], lnb_ref[...]))
    mix = _dot(v.astype(BF16), wout_ref[...])
    o_ref[0] = _layer_norm(DN_ALPHA * x + mix, g_ref[...], b_ref[...])


def _conformer_ln(x, w_in, w_dw, ln_g, ln_b, w_out, g, b):
    bn, s, d = x.shape
    tile = pl.BlockSpec((1, CONV_TS, d), lambda i, j: (i, j, 0))
    return pl.pallas_call(
        _conf_kernel,
        grid=(bn, s // CONV_TS),
        in_specs=[tile, _const_spec(w_in.shape), _const_spec(w_dw.shape), _const_spec((1, d)),
                  _const_spec((1, d)), _const_spec(w_out.shape), _const_spec((1, d)), _const_spec((1, d))],
        out_specs=tile,
        out_shape=jax.ShapeDtypeStruct(x.shape, F32),
        scratch_shapes=[pltpu.VMEM((CONF_HALO + CONV_TS, d), F32),
                        pltpu.VMEM((SUBLANES - 1, CONF_HALO + CONV_TS, d), F32), pltpu.VMEM((CONV_TS, d), F32)],
        compiler_params=_cparams("parallel", "arbitrary"),
        name="conformer_ln",
    )(x, w_in.astype(BF16), w_dw, ln_g.reshape(1, d), ln_b.reshape(1, d), w_out.astype(BF16),
      g.reshape(1, d), b.reshape(1, d))


SCONV_HALO = SUBLANES


def _sconv_kernel(x_ref, win_ref, wc_ref, wout_ref, g_ref, b_ref, o_ref, buf_ref):
    d = D_MODEL
    x = x_ref[0]
    bch = _dot(x.astype(BF16), win_ref[...])
    cv = _causal_dwconv(buf_ref, bch[:, d:2 * d] * bch[:, 2 * d:], wc_ref, SCONV_KERNEL, SCONV_HALO,
                        CONV_TS, pl.program_id(1) == 0)
    y = bch[:, :d] * cv
    mix = _dot(y.astype(BF16), wout_ref[...])
    o_ref[0] = _layer_norm(DN_ALPHA * x + mix, g_ref[...], b_ref[...])


def _sconv_ln(x, w_in, w_conv, w_out, g, b):
    bn, s, d = x.shape
    tile = pl.BlockSpec((1, CONV_TS, d), lambda i, j: (i, j, 0))
    return pl.pallas_call(
        _sconv_kernel,
        grid=(bn, s // CONV_TS),
        in_specs=[tile, _const_spec(w_in.shape), _const_spec(w_conv.shape), _const_spec(w_out.shape),
                  _const_spec((1, d)), _const_spec((1, d))],
        out_specs=tile,
        out_shape=jax.ShapeDtypeStruct(x.shape, F32),
        scratch_shapes=[pltpu.VMEM((SCONV_HALO + CONV_TS, d), F32)],
        compiler_params=_cparams("parallel", "arbitrary"),
        name="sconv_ln",
    )(x, w_in.astype(BF16), w_conv, w_out.astype(BF16), g.reshape(1, d), b.reshape(1, d))


GDN_K_HEADS = D_MODEL // 128
GDN_V_HEADS = 2 * GDN_K_HEADS
GDN_DK = 128
GDN_DV = 128
GDN_CONV = 4
GDN_CHUNK = 64
GDN_NK = GDN_K_HEADS * GDN_DK
GDN_NV = GDN_V_HEADS * GDN_DV
GDN_QKV = 2 * GDN_NK + GDN_NV
GDN_TS = 512
GDN_TC = 256
GDN_TP = 512
GDN_HALO = SUBLANES
HIGHEST = lax.Precision.HIGHEST
NT_DIMS = (((1,), (1,)), ((), ()))
TN_DIMS = (((0,), (0,)), ((), ()))


def _hdot(a, b):
    return jnp.dot(a, b, preferred_element_type=F32, precision=HIGHEST)


def _softplus(v):
    return jnp.maximum(v, 0.0) + jnp.log1p(jnp.exp(-jnp.abs(v)))


def _gdn_in_kernel(x_ref, wqkv_ref, wconv_ref, wz_ref, wba_ref, wbat_ref, alog_r_ref, dt_r_ref, alog_c_ref,
                   dt_c_ref, q_ref, k_ref, v_ref, z_ref, bcol_ref, gcol_ref, rs_ref, egl_ref, buf_ref):
    ts = GDN_TS
    xb = x_ref[0].astype(BF16)
    first = pl.program_id(1) == 0

    def conv_silu(c0, c1):
        return _silu(_causal_dwconv(buf_ref.at[:, c0:c1], _dot(xb, wqkv_ref[:, c0:c1]), wconv_ref.at[:, c0:c1],
                                    GDN_CONV, GDN_HALO, ts, first))

    for part, (o_ref, scale) in enumerate(((q_ref, GDN_DK ** -0.5), (k_ref, 1.0))):
        qk = conv_silu(part * GDN_NK, (part + 1) * GDN_NK)
        for h in range(GDN_K_HEADS):
            hd = qk[:, h * GDN_DK:(h + 1) * GDN_DK]
            o_ref[0, :, h * GDN_DK:(h + 1) * GDN_DK] = (
                hd * (lax.rsqrt(jnp.sum(hd * hd, -1, keepdims=True) + RMS_EPS) * scale))
    for part in range(GDN_NV // GDN_NK):
        v_ref[0, :, part * GDN_NK:(part + 1) * GDN_NK] = conv_silu((2 + part) * GDN_NK, (3 + part) * GDN_NK)
    z_ref[0] = _dot(xb, wz_ref[...]).astype(z_ref.dtype)

    nh = GDN_V_HEADS
    ba = _dot(xb, wba_ref[...])
    beta_c = jax.nn.sigmoid(ba[:, :nh])
    g_c = -jnp.exp(alog_r_ref[...]) * _softplus(ba[:, nh:] + dt_r_ref[...])
    bat = lax.dot_general(wbat_ref[...], xb, NT_DIMS, preferred_element_type=F32)
    beta_r = jax.nn.sigmoid(bat[:nh])
    g_r = -jnp.exp(alog_c_ref[...]) * _softplus(bat[nh:] + dt_c_ref[...])
    ri = lax.broadcasted_iota(jnp.int32, (ts, ts), 0)
    ci = lax.broadcasted_iota(jnp.int32, (ts, ts), 1)
    same = (ri // GDN_CHUNK) == (ci // GDN_CHUNK)
    low = jnp.where(same & (ci <= ri), 1.0, 0.0).astype(F32)
    upp = jnp.where(same & (ri <= ci), 1.0, 0.0).astype(F32)
    gc_c = _hdot(low, g_c)
    gc_r = _hdot(g_r, upp)
    bcol_ref[0] = beta_c
    gcol_ref[0] = gc_c
    rs_ref[0, 0:nh, :] = beta_r
    rs_ref[0, nh:2 * nh, :] = gc_r
    nchunk = ts // GDN_CHUNK
    cr = lax.broadcasted_iota(jnp.int32, (nchunk, ts), 0)
    ct = lax.broadcasted_iota(jnp.int32, (nchunk, ts), 1)
    chunk_sum = jnp.where(ct // GDN_CHUNK == cr, 1.0, 0.0).astype(F32)
    gl = _hdot(chunk_sum, g_c)
    er = lax.broadcasted_iota(jnp.int32, (nh, GDN_NV), 0)
    ec = lax.broadcasted_iota(jnp.int32, (nh, GDN_NV), 1)
    expand = jnp.where(ec // GDN_DV == er, 1.0, 0.0).astype(F32)
    egl_ref[0] = jnp.exp(_hdot(gl, expand))


def _split_bf16(v):
    hi = v.astype(BF16)
    return hi, (v - hi.astype(F32)).astype(BF16)


def _dot_x3(ah, al, bh, bl):
    return _dot(ah, bh) + (_dot(al, bh) + _dot(ah, bl))


def _unit_lower_inverses(mats, eye, order):
    n = mats[0].shape[0]
    ts = [eye - a for a in mats]
    ps = [_dot_x3(ah, al, ah, al) for ah, al in map(_split_bf16, mats)]
    for _ in range(order.bit_length() - 3):
        tsp, psp = list(map(_split_bf16, ts)), list(map(_split_bf16, ps))
        tps = [_dot_x3(jnp.concatenate([th, ph], axis=0), jnp.concatenate([tl, pl_], axis=0), ph, pl_)
               for (th, tl), (ph, pl_) in zip(tsp, psp)]
        ts = [t + tp[:n] for t, tp in zip(ts, tps)]
        ps = [tp[n:] for tp in tps]
    tsp, psp = list(map(_split_bf16, ts)), list(map(_split_bf16, ps))
    return [t + _dot_x3(th, tl, ph, pl_) for t, (th, tl), (ph, pl_) in zip(ts, tsp, psp)]


def _gdn_pre_kernel(q_ref, k_ref, v_ref, bcol_ref, gcol_ref, rs_ref, w_ref, u_ref, qh_ref, kt_ref, in_ref):
    tc, c, nh = GDN_TC, GDN_CHUNK, GDN_V_HEADS
    j = pl.program_id(1)
    head_lane = lax.broadcasted_iota(jnp.int32, (tc, nh), 1)
    ri = lax.broadcasted_iota(jnp.int32, (tc, tc), 0)
    ci = lax.broadcasted_iota(jnp.int32, (tc, tc), 1)
    same = (ri // c) == (ci // c)
    tril = same & (ri >= ci)
    strict = same & (ri > ci)
    eye = jnp.where(ri == ci, 1.0, 0.0).astype(F32)
    lane = lax.broadcasted_iota(jnp.int32, (1, LANES), 1)

    chains = []
    for g in range(GDN_TP // tc):
        rows = slice(g * tc, (g + 1) * tc)
        q, k = q_ref[0, rows, :], k_ref[0, rows, :]
        qb, kb = q.astype(BF16), k.astype(BF16)
        kk = lax.dot_general(kb, kb, NT_DIMS, preferred_element_type=F32)
        qk = lax.dot_general(qb, kb, NT_DIMS, preferred_element_type=F32)
        for hh in range(2):
            h = 2 * j + hh
            sel = head_lane == h
            beta_c = jnp.sum(jnp.where(sel, bcol_ref[0, rows, :], 0.0), -1, keepdims=True)
            gc_c = jnp.sum(jnp.where(sel, gcol_ref[0, rows, :], 0.0), -1, keepdims=True)
            gl_c = jnp.concatenate([jnp.broadcast_to(gc_c[(m + 1) * c - 1:(m + 1) * c, :], (c, 1))
                                    for m in range(tc // c)], axis=0)
            beta_r = rs_ref[0, pl.ds(h, 1), rows]
            gc_r = rs_ref[0, pl.ds(nh + h, 1), rows]
            decay = jnp.exp(jnp.where(tril, gc_c - gc_r, -jnp.inf))
            chains.append(dict(a=jnp.where(strict, beta_c * kk * decay, 0.0), rows=rows, hh=hh, q=q, k=k, kb=kb,
                               qk=qk, decay=decay, beta_r=beta_r, gc_r=gc_r, gc_c=gc_c, gl_c=gl_c))
    inverses = _unit_lower_inverses([ch["a"] for ch in chains], eye, c)

    intra_prev = None
    for ch, tm in zip(chains, inverses):
        rows, hh = ch["rows"], ch["hh"]
        cols = slice(hh * GDN_DV, (hh + 1) * GDN_DV)
        vh = v_ref[0, rows, cols].astype(BF16)
        u_ref[0, rows, cols] = _dot((tm * ch["beta_r"]).astype(BF16), vh)
        w_ref[0, rows, cols] = _dot((tm * (ch["beta_r"] * jnp.exp(ch["gc_r"]))).astype(BF16), ch["kb"]).astype(BF16)
        qh_ref[0, rows, cols] = (ch["q"] * jnp.exp(ch["gc_c"])).astype(BF16)
        kt_ref[0, rows, cols] = (ch["k"] * jnp.exp(ch["gl_c"] - ch["gc_c"])).astype(BF16)
        intra = jnp.where(tril, ch["qk"] * ch["decay"], 0.0)
        half = intra[:, :LANES] + intra[:, LANES:]
        full = half + pltpu.roll(half, c, axis=1)
        if hh == 1:
            in_ref[0, rows, :] = jnp.where(lane < c, intra_prev, full).astype(BF16)
        intra_prev = full


def _gdn_scan_kernel(w_ref, u_ref, qh_ref, kt_ref, in_ref, egl_ref, o_ref, s_ref):
    c = GDN_CHUNK
    pw = 2 * GDN_DV
    ri = lax.broadcasted_iota(jnp.int32, (pw, pw), 0)
    ci = lax.broadcasted_iota(jnp.int32, (pw, pw), 1)
    block_diag = (ri // GDN_DK) == (ci // GDN_DV)

    @pl.when(pl.program_id(1) == 0)
    def _():
        s_ref[...] = jnp.zeros(s_ref.shape, F32)

    def step(n, carry):
        r0 = pl.multiple_of(n * c, c)
        rows = pl.ds(r0, c)
        zero = jnp.zeros((c, GDN_DV), BF16)
        pairs = range(GDN_K_HEADS)
        cols = [slice(p * pw, (p + 1) * pw) for p in pairs]
        ss = [s_ref[p] for p in pairs]
        rs = [_dot(jnp.concatenate([w_ref[0, rows, cols[p]], qh_ref[0, rows, cols[p]]], axis=0),
                   ss[p].astype(BF16)) for p in pairs]
        vns = [(u_ref[0, rows, cols[p]] - rs[p][:c]).astype(BF16) for p in pairs]
        vbds = [jnp.concatenate([jnp.concatenate([vn[:, :GDN_DV], zero], axis=1),
                                 jnp.concatenate([zero, vn[:, GDN_DV:]], axis=1)], axis=0) for vn in vns]
        outs = [rs[p][c:] + _dot(in_ref[0, rows, p * 2 * c:(p + 1) * 2 * c], vbds[p]) for p in pairs]
        upds = [lax.dot_general(kt_ref[0, rows, cols[p]], vns[p], TN_DIMS, preferred_element_type=F32)
                for p in pairs]
        o_ref[0, rows, :] = jnp.concatenate(outs, axis=1)
        s_ref[...] = jnp.stack([ss[p] * egl_ref[0, pl.ds(n, 1), cols[p]] + jnp.where(block_diag, upds[p], 0.0)
                                for p in pairs], axis=0)
        return carry

    lax.fori_loop(0, GDN_TS // c, step, 0)


def _gdn_out_kernel(o_ref, z_ref, x_ref, ng_ref, wout_ref, g_ref, b_ref, y_ref):
    parts = []
    for h in range(GDN_V_HEADS):
        cols = slice(h * GDN_DV, (h + 1) * GDN_DV)
        oh = o_ref[0, :, cols]
        oh = oh * lax.rsqrt(jnp.mean(oh * oh, -1, keepdims=True) + RMS_EPS) * ng_ref[...]
        parts.append((oh * _silu(z_ref[0, :, cols].astype(F32))).astype(BF16))
    mix = _dot(jnp.concatenate(parts, axis=1), wout_ref[...])
    y_ref[0] = _layer_norm(DN_ALPHA * x_ref[0] + mix, g_ref[...], b_ref[...])


def _gdn_ln(x, w_in, w_conv, a_log, dt_bias, norm_g, w_out, g, b):
    bn, s, d = x.shape
    nh, ts = GDN_V_HEADS, GDN_TS
    nchunks = s // GDN_CHUNK
    w_qkv = w_in[:, :GDN_QKV].astype(BF16)
    w_z = w_in[:, GDN_QKV:GDN_QKV + GDN_NV].astype(BF16)
    w_ba = w_in[:, GDN_QKV + GDN_NV:].astype(BF16)
    tile = lambda w: pl.BlockSpec((1, ts, w), lambda i, jj: (i, jj, 0))
    q, k, v, z, bcol, gcol, rs, egl = pl.pallas_call(
        _gdn_in_kernel,
        grid=(bn, s // ts),
        in_specs=[tile(d), _const_spec(w_qkv.shape), _const_spec(w_conv.shape), _const_spec(w_z.shape),
                  _const_spec(w_ba.shape), _const_spec((2 * nh, d)), _const_spec((1, nh)), _const_spec((1, nh)),
                  _const_spec((nh, 1)), _const_spec((nh, 1))],
        out_specs=[tile(GDN_NK), tile(GDN_NK), tile(GDN_NV), tile(GDN_NV), tile(nh), tile(nh),
                   pl.BlockSpec((1, 2 * nh, ts), lambda i, jj: (i, 0, jj)),
                   pl.BlockSpec((1, ts // GDN_CHUNK, GDN_NV), lambda i, jj: (i, jj, 0))],
        out_shape=[jax.ShapeDtypeStruct((bn, s, GDN_NK), F32), jax.ShapeDtypeStruct((bn, s, GDN_NK), F32),
                   jax.ShapeDtypeStruct((bn, s, GDN_NV), F32), jax.ShapeDtypeStruct((bn, s, GDN_NV), BF16),
                   jax.ShapeDtypeStruct((bn, s, nh), F32), jax.ShapeDtypeStruct((bn, s, nh), F32),
                   jax.ShapeDtypeStruct((bn, 2 * nh, s), F32),
                   jax.ShapeDtypeStruct((bn, nchunks, GDN_NV), F32)],
        scratch_shapes=[pltpu.VMEM((GDN_HALO + ts, GDN_QKV), F32)],
        compiler_params=_cparams("parallel", "arbitrary"),
        name="gdn_in",
    )(x, w_qkv, w_conv, w_z, w_ba, w_ba.T, a_log.reshape(1, nh), dt_bias.reshape(1, nh),
      a_log.reshape(nh, 1), dt_bias.reshape(nh, 1))

    tc = GDN_TP
    pair = lambda w: pl.BlockSpec((1, tc, w), lambda i, jj, tt: (i, tt, jj))
    whole = lambda w: pl.BlockSpec((1, tc, w), lambda i, jj, tt: (i, tt, 0))
    w, u, qh, kt, intra = pl.pallas_call(
        _gdn_pre_kernel,
        grid=(bn, GDN_K_HEADS, s // tc),
        in_specs=[pair(GDN_DK), pair(GDN_DK), pair(2 * GDN_DV), whole(nh), whole(nh),
                  pl.BlockSpec((1, 2 * nh, tc), lambda i, jj, tt: (i, 0, tt))],
        out_specs=[pair(2 * GDN_DK), pair(2 * GDN_DV), pair(2 * GDN_DK), pair(2 * GDN_DK), pair(2 * GDN_CHUNK)],
        out_shape=[jax.ShapeDtypeStruct((bn, s, 2 * GDN_NK), BF16), jax.ShapeDtypeStruct((bn, s, GDN_NV), F32),
                   jax.ShapeDtypeStruct((bn, s, 2 * GDN_NK), BF16), jax.ShapeDtypeStruct((bn, s, 2 * GDN_NK), BF16),
                   jax.ShapeDtypeStruct((bn, s, GDN_V_HEADS * GDN_CHUNK), BF16)],
        compiler_params=_cparams("parallel", "parallel", "parallel"),
        name="gdn_pre",
    )(q, k, v, bcol, gcol, rs)

    o = pl.pallas_call(
        _gdn_scan_kernel,
        grid=(bn, s // ts),
        in_specs=[tile(2 * GDN_NK), tile(GDN_NV), tile(2 * GDN_NK), tile(2 * GDN_NK),
                  tile(GDN_V_HEADS * GDN_CHUNK),
                  pl.BlockSpec((1, ts // GDN_CHUNK, GDN_NV), lambda i, jj: (i, jj, 0))],
        out_specs=tile(GDN_NV),
        out_shape=jax.ShapeDtypeStruct((bn, s, GDN_NV), F32),
        scratch_shapes=[pltpu.VMEM((GDN_K_HEADS, 2 * GDN_DK, 2 * GDN_DV), F32)],
        compiler_params=_cparams("parallel", "arbitrary"),
        name="gdn_scan",
    )(w, u, qh, kt, intra, egl)

    return pl.pallas_call(
        _gdn_out_kernel,
        grid=(bn, s // ts),
        in_specs=[tile(GDN_NV), tile(GDN_NV), tile(d), _const_spec((1, GDN_DV)), _const_spec(w_out.shape),
                  _const_spec((1, d)), _const_spec((1, d))],
        out_specs=tile(d),
        out_shape=jax.ShapeDtypeStruct(x.shape, F32),
        compiler_params=_cparams("parallel", "parallel"),
        name="gdn_out",
    )(o, z, x, norm_g.reshape(1, GDN_DV), w_out.astype(BF16), g.reshape(1, d), b.reshape(1, d))


ATT_HEADS = D_MODEL // 64
ATT_HEAD_DIM = 64
ROPE_DIM = ATT_HEAD_DIM // 4
ROPE_HALF = ROPE_DIM // 2
ATT_NOPE_DIM = ATT_HEAD_DIM - ROPE_DIM
Q_RANK = D_MODEL // 4
KV_RANK = D_MODEL // 8
IDX_HEADS = 8
IDX_DIM = ATT_HEAD_DIM
TOPK_MAX = 256
QK_WIDTH = 2 * LANES
DSA_TS = 512
DSA_QB = 128
DSA_KT = 512
MASKED = -1e30
SEARCH_GROUP = 4
LOG2_E = 1.4426950408889634


def _rope_tab_kernel(pos_ref, inv_ref, rot_ref, sgn_ref, cos_ref, sin_ref):
    ang = pos_ref[0].astype(F32) * inv_ref[...]
    cos_ref[0] = jnp.where(rot_ref[...] > 0.0, jnp.cos(ang), 1.0)
    sin_ref[0] = jnp.sin(ang) * sgn_ref[...]


def _rope_tables(positions):
    bn, s = positions.shape
    lane = jnp.arange(LANES)
    inv = ROPE_THETA ** (-jnp.arange(0, ROPE_DIM, 2, dtype=F32) / ROPE_DIM)
    within = lane % ATT_HEAD_DIM
    rot = (within < ROPE_DIM).astype(F32)
    sgn = jnp.where(within < ROPE_HALF, -1.0, 1.0).astype(F32) * rot
    row = lambda v: v.reshape(1, LANES)
    ts = DSA_TS
    tile = pl.BlockSpec((1, ts, LANES), lambda i, j: (i, j, 0))
    return pl.pallas_call(
        _rope_tab_kernel,
        grid=(bn, s // ts),
        in_specs=[pl.BlockSpec((1, ts, 1), lambda i, j: (i, j, 0))] + [_const_spec((1, LANES))] * 3,
        out_specs=[tile, tile],
        out_shape=[jax.ShapeDtypeStruct((bn, s, LANES), F32)] * 2,
        compiler_params=_cparams("parallel", "parallel"),
        name="rope_tables",
    )(positions.reshape(bn, s, 1), row(inv[lane % ROPE_HALF]), row(rot), row(sgn))


def _rope(v, cosm, sinm):
    lane = lax.broadcasted_iota(jnp.int32, (1, LANES), 1)
    low = (lane % ROPE_DIM) < ROPE_HALF
    cols = []
    for c0 in range(0, v.shape[1], LANES):
        blk = v[:, c0:c0 + LANES]
        partner = jnp.where(low, pltpu.roll(blk, LANES - ROPE_HALF, axis=1), pltpu.roll(blk, ROPE_HALF, axis=1))
        cols.append(blk * cosm + partner * sinm)
    return cols[0] if len(cols) == 1 else jnp.concatenate(cols, axis=1)


def _rms_norm(v, g):
    return v * lax.rsqrt(jnp.mean(v * v, -1, keepdims=True) + RMS_EPS) * g


def _dsa_proj_kernel(x_ref, cos_ref, sin_ref, wdq_ref, qn_ref, wuq_ref, wcomb_ref, wdkv_ref, kvn_ref, wkr_ref,
                     wiq_ref, wik_ref, ikg_ref, ikb_ref, prot_ref, wiwt_ref,
                     q2_ref, k2_ref, qi_ref, ki_ref, wit_ref):
    xb = x_ref[0].astype(BF16)
    cosm, sinm = cos_ref[0], sin_ref[0]
    cq = _rms_norm(_dot(xb, wdq_ref[...]), qn_ref[...]).astype(BF16)
    q = _rope(_dot(cq, wuq_ref[...]), cosm, sinm) * (ATT_HEAD_DIM ** -0.5 * LOG2_E)
    q2_ref[0] = _dot(q.astype(BF16), wcomb_ref[...]).astype(BF16)
    ckv = _rms_norm(_dot(xb, wdkv_ref[...]), kvn_ref[...])
    kr = _rope(_dot(xb, wkr_ref[...]), cosm, sinm)
    k2_ref[0] = jnp.concatenate([ckv, kr], axis=1).astype(BF16)
    qi = _rope(_dot(cq, wiq_ref[...]), cosm, sinm).astype(BF16)
    for h in range(IDX_HEADS):
        qi_ref[0, h] = qi[:, h * IDX_DIM:(h + 1) * IDX_DIM]
    kin = _layer_norm(_dot(xb, wik_ref[...]), ikg_ref[...], ikb_ref[...])
    ki = kin * cosm[:, :IDX_DIM] + _hdot(kin, prot_ref[...]) * sinm[:, :IDX_DIM]
    ki_ref[0] = ki.astype(BF16)
    wit = lax.dot_general(wiwt_ref[...], xb, NT_DIMS, preferred_element_type=F32)
    wit_ref[0] = wit * (IDX_HEADS * IDX_DIM) ** -0.5


def _dsa_attn_kernel(q2_ref, k2_ref, qi_ref, ki_ref, wit_ref, o_ref,
                     key_ref, mask_ref, q2s_ref, lga_ref, lgb_ref, pa_ref, pb_ref, acc_ref):
    qb, kt_sz = DSA_QB, DSA_KT
    i = pl.program_id(1)
    n_tiles = (i * qb) // kt_sz + 1
    q_pos = i * qb + lax.broadcasted_iota(jnp.int32, (1, qb), 1)

    def score_tile(t, carry):
        r0 = pl.multiple_of(t * kt_sz, kt_sz)
        kblk = ki_ref[0, pl.ds(r0, kt_sz), :]
        acc = jnp.zeros((kt_sz, qb), F32)
        for h in range(IDX_HEADS):
            sc = lax.dot_general(kblk, qi_ref[0, h], NT_DIMS, preferred_element_type=F32)
            acc = acc + jnp.maximum(sc, 0.0) * wit_ref[0, h:h + 1, :]
        k_pos = r0 + lax.broadcasted_iota(jnp.int32, (kt_sz, 1), 0)
        acc = jnp.where(k_pos <= q_pos, acc, -jnp.inf)
        bits = pltpu.bitcast(acc, jnp.int32)
        key_ref[pl.ds(r0, kt_sz), :] = bits ^ ((bits >> 31) & jnp.int32(0x7FFFFFFF))
        return carry

    lax.fori_loop(0, n_tiles, score_tile, 0)

    def count(pred):
        def body(t, acc):
            r0 = pl.multiple_of(t * kt_sz, kt_sz)
            k_pos = r0 + lax.broadcasted_iota(jnp.int32, (kt_sz, 1), 0)
            hit = jnp.where(pred(key_ref[pl.ds(r0, kt_sz), :], k_pos), 1, 0).astype(jnp.int32)
            return acc + jnp.sum(hit.reshape(kt_sz // SUBLANES, SUBLANES, qb), axis=0)
        acc = lax.fori_loop(0, n_tiles, body, jnp.zeros((SUBLANES, qb), jnp.int32))
        return jnp.sum(acc, axis=0, keepdims=True)

    int_min = jnp.int32(-2 ** 31)
    topk = jnp.int32(TOPK_MAX)

    def search_bits(state):
        g, thr, kept, _ = state
        for s in range(SEARCH_GROUP):
            cand = thr + lax.shift_left(jnp.int32(1), jnp.int32(31 - s) - g * SEARCH_GROUP)
            cnt = count(lambda kb, kp: kb >= cand)
            ok = cnt >= topk
            thr = jnp.where(ok, cand, thr)
            kept = jnp.where(ok, cnt, kept)
        return g + 1, thr, kept, jnp.max(jnp.where(kept != topk, 1, 0))

    def search():
        init = (jnp.int32(0), thr0, jnp.full((1, qb), jnp.int32(2 ** 30), jnp.int32), jnp.int32(1))
        return lax.while_loop(lambda st: jnp.logical_and(st[0] < 32 // SEARCH_GROUP, st[3] > 0), search_bits, init)[1]

    few_keys = (i + 1) * qb <= TOPK_MAX
    thr0 = jnp.full((1, qb), int_min, jnp.int32)
    thr = lax.cond(few_keys, lambda: thr0, search)

    need = topk - count(lambda kb, kp: kb > thr)
    n_eq = count(lambda kb, kp: kb == thr)
    full_cut = jnp.full((1, qb), jnp.int32(2 ** 30), jnp.int32)

    def search_cut():
        def cut_bit(b, cut):
            cand = cut + lax.shift_left(jnp.int32(1), jnp.int32(12) - b)
            below = count(lambda kb, kp: (kb == thr) & (kp < cand))
            return jnp.where(below < need, cand, cut)
        return lax.fori_loop(0, 13, cut_bit, jnp.zeros((1, qb), jnp.int32))

    tied = jnp.logical_and(jnp.logical_not(few_keys), jnp.max(jnp.where(n_eq != need, 1, 0)) > 0)
    cut = lax.cond(tied, search_cut, lambda: full_cut)

    def mask_tile(t, carry):
        r0 = pl.multiple_of(t * kt_sz, kt_sz)
        kb = key_ref[pl.ds(r0, kt_sz), :]
        k_pos = r0 + lax.broadcasted_iota(jnp.int32, (kt_sz, 1), 0)
        keep = ((kb > thr) | ((kb == thr) & (k_pos <= cut))) & (k_pos <= q_pos)
        mask_ref[:, pl.ds(r0, kt_sz)] = jnp.where(keep, 0.0, MASKED).astype(F32).T
        return carry

    lax.fori_loop(0, n_tiles, mask_tile, 0)

    nh = ATT_HEADS
    half = kt_sz // 2
    for h in range(nh):
        q2s_ref[h * qb:(h + 1) * qb, :] = q2_ref[0, :, h * QK_WIDTH:(h + 1) * QK_WIDTH]
    acc_ref[...] = jnp.zeros(acc_ref.shape, F32)
    vlane = lax.broadcasted_iota(jnp.int32, (1, QK_WIDTH), 1)
    ones_col = jnp.where(vlane == KV_RANK, 1.0, 0.0).astype(BF16)

    def logits_into(dst_ref, key0):
        dst_ref[...] = lax.dot_general(q2s_ref[...], k2_ref[0, pl.ds(pl.multiple_of(key0, half), half), :],
                                       NT_DIMS, preferred_element_type=F32)

    def softmax_pv(src_ref, prob_ref, key0, m):
        key0 = pl.multiple_of(key0, half)
        mask = mask_ref[:, pl.ds(key0, half)]
        wide = lambda v: jnp.concatenate([v] * (half // LANES), axis=1)
        m_parts, alpha_parts = [], []
        for h in range(nh):
            rows = slice(h * qb, (h + 1) * qb)
            lg = src_ref[rows, :] + mask
            m_new = jnp.maximum(m[rows], jnp.max(lg, -1, keepdims=True))
            prob_ref[rows, :] = jnp.exp2(lg - wide(m_new)).astype(BF16)
            alpha_parts.append(jnp.exp2(m[rows] - m_new))
            m_parts.append(m_new)
        v1 = jnp.where(vlane < KV_RANK, k2_ref[0, pl.ds(key0, half), :], ones_col)
        alpha = jnp.concatenate(alpha_parts, axis=0)
        acc_ref[...] = acc_ref[...] * jnp.concatenate([alpha] * (QK_WIDTH // LANES), axis=1) + _dot(prob_ref[...], v1)
        return jnp.concatenate(m_parts, axis=0)

    logits_into(lga_ref, 0)
    last_key0 = (n_tiles - 1) * kt_sz

    def attn_tile(t, m):
        key0 = t * kt_sz
        logits_into(lgb_ref, key0 + half)
        m = softmax_pv(lga_ref, pa_ref, key0, m)
        logits_into(lga_ref, jnp.minimum(key0 + kt_sz, last_key0))
        return softmax_pv(lgb_ref, pb_ref, key0 + half, m)

    lax.fori_loop(0, n_tiles, attn_tile, jnp.full((nh * qb, LANES), MASKED, F32))
    for h in range(nh):
        acc = acc_ref[h * qb:(h + 1) * qb, :]
        o_ref[0, :, h * KV_RANK:(h + 1) * KV_RANK] = (
            acc[:, :KV_RANK] / acc[:, KV_RANK:KV_RANK + 1]).astype(o_ref.dtype)


def _dsa_out_kernel(o_ref, x_ref, wuv_ref, wo_ref, g_ref, b_ref, y_ref):
    o = _dot(o_ref[0], wuv_ref[...]).astype(BF16)
    mix = _dot(o, wo_ref[...])
    y_ref[0] = _layer_norm(DN_ALPHA * x_ref[0] + mix, g_ref[...], b_ref[...])


def _block_diag(blocks):
    n, r, c = blocks.shape
    idx = jnp.arange(n)
    return jnp.zeros((n, r, n, c), blocks.dtype).at[idx, :, idx, :].set(blocks).reshape(n * r, n * c)


def _dsa_ln(x, positions, w_dq, q_norm, w_uq, w_dkv, kv_norm, w_kr, w_uk, w_uv, w_o,
            w_iq, w_ik, ik_g, ik_b, w_iw, g, b):
    bn, s, d = x.shape
    ts, nh = DSA_TS, ATT_HEADS
    cosm, sinm = _rope_tables(positions)

    head_map = jnp.zeros((nh, ATT_HEAD_DIM, QK_WIDTH), F32)
    head_map = head_map.at[:, ROPE_DIM:, :KV_RANK].set(w_uk)
    head_map = head_map.at[:, :ROPE_DIM, KV_RANK:KV_RANK + ROPE_DIM].set(jnp.eye(ROPE_DIM, dtype=F32))
    w_comb = _block_diag(head_map).astype(BF16)
    w_kr_pad = jnp.zeros((d, LANES), F32).at[:, :ROPE_DIM].set(w_kr).astype(BF16)
    lane = jnp.arange(IDX_DIM)
    partner = jnp.where(lane < ROPE_HALF, lane + ROPE_HALF, lane - ROPE_HALF)
    p_rot = ((lane[:, None] == partner[None, :]) & (lane[None, :] < ROPE_DIM)).astype(F32)

    tile = lambda w: pl.BlockSpec((1, ts, w), lambda i, j: (i, j, 0))
    consts = [w_dq.astype(BF16), q_norm.reshape(1, -1), w_uq.astype(BF16), w_comb, w_dkv.astype(BF16),
              kv_norm.reshape(1, -1), w_kr_pad, w_iq.astype(BF16), w_ik.astype(BF16), ik_g.reshape(1, -1),
              ik_b.reshape(1, -1), p_rot, w_iw.T.astype(BF16)]
    q2, k2, qi, ki, wit = pl.pallas_call(
        _dsa_proj_kernel,
        grid=(bn, s // ts),
        in_specs=[tile(d), tile(LANES), tile(LANES)] + [_const_spec(c.shape) for c in consts],
        out_specs=[tile(nh * QK_WIDTH), tile(QK_WIDTH),
                   pl.BlockSpec((1, IDX_HEADS, ts, IDX_DIM), lambda i, j: (i, 0, j, 0)),
                   tile(IDX_DIM), pl.BlockSpec((1, IDX_HEADS, ts), lambda i, j: (i, 0, j))],
        out_shape=[jax.ShapeDtypeStruct((bn, s, nh * QK_WIDTH), BF16), jax.ShapeDtypeStruct((bn, s, QK_WIDTH), BF16),
                   jax.ShapeDtypeStruct((bn, IDX_HEADS, s, IDX_DIM), BF16),
                   jax.ShapeDtypeStruct((bn, s, IDX_DIM), BF16), jax.ShapeDtypeStruct((bn, IDX_HEADS, s), F32)],
        compiler_params=_cparams("parallel", "parallel"),
        name="dsa_proj",
    )(x, cosm, sinm, *consts)

    qb = DSA_QB
    o_lat = pl.pallas_call(
        _dsa_attn_kernel,
        grid=(bn, s // qb),
        in_specs=[pl.BlockSpec((1, qb, nh * QK_WIDTH), lambda i, j: (i, j, 0)),
                  pl.BlockSpec((1, s, QK_WIDTH), lambda i, j: (i, 0, 0)),
                  pl.BlockSpec((1, IDX_HEADS, qb, IDX_DIM), lambda i, j: (i, 0, j, 0)),
                  pl.BlockSpec((1, s, IDX_DIM), lambda i, j: (i, 0, 0)),
                  pl.BlockSpec((1, IDX_HEADS, qb), lambda i, j: (i, 0, j))],
        out_specs=pl.BlockSpec((1, qb, nh * KV_RANK), lambda i, j: (i, j, 0)),
        out_shape=jax.ShapeDtypeStruct((bn, s, nh * KV_RANK), BF16),
        scratch_shapes=[pltpu.VMEM((s, qb), jnp.int32), pltpu.VMEM((qb, s), F32),
                        pltpu.VMEM((nh * qb, QK_WIDTH), BF16),
                        pltpu.VMEM((nh * qb, DSA_KT // 2), F32), pltpu.VMEM((nh * qb, DSA_KT // 2), F32),
                        pltpu.VMEM((nh * qb, DSA_KT // 2), BF16), pltpu.VMEM((nh * qb, DSA_KT // 2), BF16),
                        pltpu.VMEM((nh * qb, QK_WIDTH), F32)],
        compiler_params=_cparams("parallel", "parallel"),
        name="dsa_attn",
    )(q2, k2, qi, ki, wit)

    w_uv_bd = _block_diag(w_uv).astype(BF16)
    return pl.pallas_call(
        _dsa_out_kernel,
        grid=(bn, s // ts),
        in_specs=[tile(nh * KV_RANK), tile(d), _const_spec(w_uv_bd.shape), _const_spec(w_o.shape),
                  _const_spec((1, d)), _const_spec((1, d))],
        out_specs=tile(d),
        out_shape=jax.ShapeDtypeStruct(x.shape, F32),
        compiler_params=_cparams("parallel", "parallel"),
        name="dsa_out",
    )(o_lat, x, w_uv_bd, w_o.astype(BF16), g.reshape(1, d), b.reshape(1, d))


def kernel(x, positions, ln_g, ln_b, ffn_w_gate, ffn_w_up, ffn_w_down,
           conv_w_in, conv_w_dw, conv_ln_g, conv_ln_b, conv_w_out,
           sc_w_in, sc_w_conv, sc_w_out,
           dsa_w_dq, dsa_q_norm, dsa_w_uq, dsa_w_dkv, dsa_kv_norm, dsa_w_kr,
           dsa_w_uk, dsa_w_uv, dsa_w_o, dsa_w_iq, dsa_w_ik, dsa_ik_ln_g, dsa_ik_ln_b, dsa_w_iw,
           gdn_w_in, gdn_w_conv, gdn_a_log, gdn_dt_bias, gdn_norm_g, gdn_w_out):
    bn, s, d = x.shape

    def ffn(h, i, half):
        return _ffn_ln(h.reshape(bn * s, d), ffn_w_gate[i, half], ffn_w_up[i, half], ffn_w_down[i, half],
                       ln_g[i, 2 * half], ln_b[i, 2 * half]).reshape(bn, s, d)

    for i in range(DEPTH):
        m, j = i % N_MIXERS, i // N_MIXERS
        x = ffn(x, i, 0)
        g, b = ln_g[i, 1], ln_b[i, 1]
        if m == 0:
            x = _conformer_ln(x, conv_w_in[j], conv_w_dw[j], conv_ln_g[j], conv_ln_b[j], conv_w_out[j], g, b)
        elif m == 1:
            x = _sconv_ln(x, sc_w_in[j], sc_w_conv[j], sc_w_out[j], g, b)
        elif m == 2:
            x = _dsa_ln(x, positions, dsa_w_dq[j], dsa_q_norm[j], dsa_w_uq[j], dsa_w_dkv[j], dsa_kv_norm[j],
                        dsa_w_kr[j], dsa_w_uk[j], dsa_w_uv[j], dsa_w_o[j], dsa_w_iq[j], dsa_w_ik[j],
                        dsa_ik_ln_g[j], dsa_ik_ln_b[j], dsa_w_iw[j], g, b)
        else:
            x = _gdn_ln(x, gdn_w_in[j], gdn_w_conv[j], gdn_a_log[j], gdn_dt_bias[j], gdn_norm_g[j],
                        gdn_w_out[j], g, b)
        x = ffn(x, i, 1)
    return x
```

```python
import functools

import jax
import jax.numpy as jnp
from jax import lax
from jax.experimental import pallas as pl
from jax.experimental.pallas import tpu as pltpu

D_MODEL = 1024
DEPTH = 4
N_MIXERS = 4
FFN_DIM = ((8 * D_MODEL // 3 + 127) // 128) * 128
DN_ALPHA = (2.0 * DEPTH) ** 0.25
ROPE_THETA = 500000.0
LN_EPS = 1e-5
RMS_EPS = 1e-6
CONF_KERNEL = 31
SCONV_KERNEL = 3

BF16 = jnp.bfloat16
F32 = jnp.float32

V7X_VMEM_LIMIT_BYTES = 56 * 1024 * 1024
SUBLANES = 8
LANES = 128


def _cparams(*sem):
    return pltpu.CompilerParams(dimension_semantics=sem, vmem_limit_bytes=V7X_VMEM_LIMIT_BYTES)


def _layer_norm(y, g, b):
    mu = jnp.mean(y, -1, keepdims=True)
    yc = y - mu
    var = jnp.mean(yc * yc, -1, keepdims=True)
    return yc * lax.rsqrt(var + LN_EPS) * g + b


def _silu(h):
    return h * jax.nn.sigmoid(h)


def _dot(a, b):
    return jnp.dot(a, b, preferred_element_type=F32)


def _const_spec(shape):
    nd = len(shape)
    return pl.BlockSpec(shape, lambda *_: (0,) * nd, pipeline_mode=pl.Buffered(1))


FFN_TM = 512
FFN_CHUNK = 768


def _ffn_kernel(x_ref, wg_ref, wu_ref, wd_ref, g_ref, b_ref, o_ref):
    x = x_ref[...]
    xb = x.astype(BF16)
    bounds = list(range(0, FFN_DIM, FFN_CHUNK)) + [FFN_DIM]
    gate_up = lambda c: (_dot(xb, wg_ref[:, bounds[c]:bounds[c + 1]]), _dot(xb, wu_ref[:, bounds[c]:bounds[c + 1]]))
    acc = jnp.zeros(x.shape, F32)
    h, u = gate_up(0)
    for c in range(len(bounds) - 1):
        nxt = gate_up(c + 1) if c + 2 < len(bounds) else None
        a = (_silu(h) * u).astype(BF16)
        acc = acc + _dot(a, wd_ref[bounds[c]:bounds[c + 1], :])
        if nxt is not None:
            h, u = nxt
    o_ref[...] = _layer_norm(DN_ALPHA * x + 0.5 * acc, g_ref[...], b_ref[...])


def _ffn_ln(x2, wg, wu, wd, g, b):
    t, d = x2.shape
    row = pl.BlockSpec((FFN_TM, d), lambda i: (i, 0))
    return pl.pallas_call(
        _ffn_kernel,
        grid=(t // FFN_TM,),
        in_specs=[row, _const_spec(wg.shape), _const_spec(wu.shape), _const_spec(wd.shape),
                  _const_spec((1, d)), _const_spec((1, d))],
        out_specs=row,
        out_shape=jax.ShapeDtypeStruct((t, d), F32),
        compiler_params=_cparams("parallel"),
        name="ffn_ln",
    )(x2, wg.astype(BF16), wu.astype(BF16), wd.astype(BF16), g.reshape(1, d), b.reshape(1, d))


def _causal_dwconv(buf_ref, u, w_ref, width, halo, ts, first_tile):
    c = u.shape[-1]

    @pl.when(first_tile)
    def _():
        buf_ref[0:halo, :] = jnp.zeros((halo, c), F32)

    @pl.when(jnp.logical_not(first_tile))
    def _():
        buf_ref[0:halo, :] = buf_ref[ts:ts + halo, :]

    buf_ref[halo:halo + ts, :] = u
    off = halo - (width - 1)
    acc = w_ref[width - 1:width, :] * u
    for k in range(width - 1):
        acc = acc + w_ref[k:k + 1, :] * buf_ref[off + k:off + k + ts, :]
    return acc


def _causal_dwconv_wide(buf_ref, sh_ref, out_ref, u, w_ref, width, halo, ts, first_tile):
    c = u.shape[-1]

    @pl.when(first_tile)
    def _():
        buf_ref[0:halo, :] = jnp.zeros((halo, c), F32)

    @pl.when(jnp.logical_not(first_tile))
    def _():
        buf_ref[0:halo, :] = buf_ref[ts:ts + halo, :]

    buf_ref[halo:halo + ts, :] = u
    off = halo - (width - 1)
    n = halo + ts - SUBLANES
    for b in range(1, SUBLANES):
        sh_ref[b - 1, 0:n, :] = buf_ref[b:b + n, :]

    def block(i, carry):
        r = pl.multiple_of(i * DWCONV_ROWS, DWCONV_ROWS)
        for c0 in range(0, c, DWCONV_COLS):
            cols = slice(c0, c0 + DWCONV_COLS)
            acc = w_ref[width - 1:width, cols] * buf_ref[pl.ds(halo + r, DWCONV_ROWS), cols]
            for k in range(width - 1):
                a, b = divmod(off + k, SUBLANES)
                src = buf_ref if b == 0 else sh_ref.at[b - 1]
                acc = acc + w_ref[k:k + 1, cols] * src[pl.ds(a * SUBLANES + r, DWCONV_ROWS), cols]
            out_ref[pl.ds(r, DWCONV_ROWS), cols] = acc
        return carry

    lax.fori_loop(0, ts // DWCONV_ROWS, block, 0)
    return out_ref[...]


CONV_TS = 512
CONF_HALO = 32
DWCONV_ROWS = 64
DWCONV_COLS = 256


def _conf_kernel(x_ref, win_ref, wdw_ref, lng_ref, lnb_ref, wout_ref, g_ref, b_ref, o_ref, buf_ref, sh_ref, cv_ref):
    d = D_MODEL
    x = x_ref[0]
    uu = _dot(x.astype(BF16), win_ref[...])
    u = uu[:, :d] * jax.nn.sigmoid(uu[:, d:])
    cv = _causal_dwconv_wide(buf_ref, sh_ref, cv_ref, u, wdw_ref, CONF_KERNEL, CONF_HALO, CONV_TS,
                             pl.program_id(1) == 0)
    v = _silu(_layer_norm(cv, lng_ref[...], lnb_ref[...]))
    mix = _dot(v.astype(BF16), wout_ref[...])
    o_ref[0] = _layer_norm(DN_ALPHA * x + mix, g_ref[...], b_ref[...])


def _conformer_ln(x, w_in, w_dw, ln_g, ln_b, w_out, g, b):
    bn, s, d = x.shape
    tile = pl.BlockSpec((1, CONV_TS, d), lambda i, j: (i, j, 0))
    return pl.pallas_call(
        _conf_kernel,
        grid=(bn, s // CONV_TS),
        in_specs=[tile, _const_spec(w_in.shape), _const_spec(w_dw.shape), _const_spec((1, d)),
                  _const_spec((1, d)), _const_spec(w_out.shape), _const_spec((1, d)), _const_spec((1, d))],
        out_specs=tile,
        out_shape=jax.ShapeDtypeStruct(x.shape, F32),
        scratch_shapes=[pltpu.VMEM((CONF_HALO + CONV_TS, d), F32),
                        pltpu.VMEM((SUBLANES - 1, CONF_HALO + CONV_TS, d), F32), pltpu.VMEM((CONV_TS, d), F32)],
        compiler_params=_cparams("parallel", "arbitrary"),
        name="conformer_ln",
    )(x, w_in.astype(BF16), w_dw, ln_g.reshape(1, d), ln_b.reshape(1, d), w_out.astype(BF16),
      g.reshape(1, d), b.reshape(1, d))


SCONV_HALO = SUBLANES


def _sconv_kernel(x_ref, win_ref, wc_ref, wout_ref, g_ref, b_ref, o_ref, buf_ref):
    d = D_MODEL
    x = x_ref[0]
    bch = _dot(x.astype(BF16), win_ref[...])
    cv = _causal_dwconv(buf_ref, bch[:, d:2 * d] * bch[:, 2 * d:], wc_ref, SCONV_KERNEL, SCONV_HALO,
                        CONV_TS, pl.program_id(1) == 0)
    y = bch[:, :d] * cv
    mix = _dot(y.astype(BF16), wout_ref[...])
    o_ref[0] = _layer_norm(DN_ALPHA * x + mix, g_ref[...], b_ref[...])


def _sconv_ln(x, w_in, w_conv, w_out, g, b):
    bn, s, d = x.shape
    tile = pl.BlockSpec((1, CONV_TS, d), lambda i, j: (i, j, 0))
    return pl.pallas_call(
        _sconv_kernel,
        grid=(bn, s // CONV_TS),
        in_specs=[tile, _const_spec(w_in.shape), _const_spec(w_conv.shape), _const_spec(w_out.shape),
                  _const_spec((1, d)), _const_spec((1, d))],
        out_specs=tile,
        out_shape=jax.ShapeDtypeStruct(x.shape, F32),
        scratch_shapes=[pltpu.VMEM((SCONV_HALO + CONV_TS, d), F32)],
        compiler_params=_cparams("parallel", "arbitrary"),
        name="sconv_ln",
    )(x, w_in.astype(BF16), w_conv, w_out.astype(BF16), g.reshape(1, d), b.reshape(1, d))


GDN_K_HEADS = D_MODEL // 128
GDN_V_HEADS = 2 * GDN_K_HEADS
GDN_DK = 128
GDN_DV = 128
GDN_CONV = 4
GDN_CHUNK = 64
GDN_NK = GDN_K_HEADS * GDN_DK
GDN_NV = GDN_V_HEADS * GDN_DV
GDN_QKV = 2 * GDN_NK + GDN_NV
GDN_TS = 512
GDN_TC = 256
GDN_TP = 512
GDN_HALO = SUBLANES
HIGHEST = lax.Precision.HIGHEST
NT_DIMS = (((1,), (1,)), ((), ()))
TN_DIMS = (((0,), (0,)), ((), ()))


def _hdot(a, b):
    return jnp.dot(a, b, preferred_element_type=F32, precision=HIGHEST)


def _softplus(v):
    return jnp.maximum(v, 0.0) + jnp.log1p(jnp.exp(-jnp.abs(v)))


def _gdn_in_kernel(x_ref, wqkv_ref, wconv_ref, wz_ref, wba_ref, wbat_ref, alog_r_ref, dt_r_ref, alog_c_ref,
                   dt_c_ref, q_ref, k_ref, v_ref, z_ref, bcol_ref, gcol_ref, rs_ref, egl_ref, buf_ref):
    ts = GDN_TS
    xb = x_ref[0].astype(BF16)
    first = pl.program_id(1) == 0

    def conv_silu(c0, c1):
        return _silu(_causal_dwconv(buf_ref.at[:, c0:c1], _dot(xb, wqkv_ref[:, c0:c1]), wconv_ref.at[:, c0:c1],
                                    GDN_CONV, GDN_HALO, ts, first))

    for part, (o_ref, scale) in enumerate(((q_ref, GDN_DK ** -0.5), (k_ref, 1.0))):
        qk = conv_silu(part * GDN_NK, (part + 1) * GDN_NK)
        for h in range(GDN_K_HEADS):
            hd = qk[:, h * GDN_DK:(h + 1) * GDN_DK]
            o_ref[0, :, h * GDN_DK:(h + 1) * GDN_DK] = (
                hd * (lax.rsqrt(jnp.sum(hd * hd, -1, keepdims=True) + RMS_EPS) * scale))
    for part in range(GDN_NV // GDN_NK):
        v_ref[0, :, part * GDN_NK:(part + 1) * GDN_NK] = conv_silu((2 + part) * GDN_NK, (3 + part) * GDN_NK)
    z_ref[0] = _dot(xb, wz_ref[...]).astype(z_ref.dtype)

    nh = GDN_V_HEADS
    ba = _dot(xb, wba_ref[...])
    beta_c = jax.nn.sigmoid(ba[:, :nh])
    g_c = -jnp.exp(alog_r_ref[...]) * _softplus(ba[:, nh:] + dt_r_ref[...])
    bat = lax.dot_general(wbat_ref[...], xb, NT_DIMS, preferred_element_type=F32)
    beta_r = jax.nn.sigmoid(bat[:nh])
    g_r = -jnp.exp(alog_c_ref[...]) * _softplus(bat[nh:] + dt_c_ref[...])
    ri = lax.broadcasted_iota(jnp.int32, (ts, ts), 0)
    ci = lax.broadcasted_iota(jnp.int32, (ts, ts), 1)
    same = (ri // GDN_CHUNK) == (ci // GDN_CHUNK)
    low = jnp.where(same & (ci <= ri), 1.0, 0.0).astype(F32)
    upp = jnp.where(same & (ri <= ci), 1.0, 0.0).astype(F32)
    gc_c = _hdot(low, g_c)
    gc_r = _hdot(g_r, upp)
    bcol_ref[0] = beta_c
    gcol_ref[0] = gc_c
    rs_ref[0, 0:nh, :] = beta_r
    rs_ref[0, nh:2 * nh, :] = gc_r
    nchunk = ts // GDN_CHUNK
    cr = lax.broadcasted_iota(jnp.int32, (nchunk, ts), 0)
    ct = lax.broadcasted_iota(jnp.int32, (nchunk, ts), 1)
    chunk_sum = jnp.where(ct // GDN_CHUNK == cr, 1.0, 0.0).astype(F32)
    gl = _hdot(chunk_sum, g_c)
    er = lax.broadcasted_iota(jnp.int32, (nh, GDN_NV), 0)
    ec = lax.broadcasted_iota(jnp.int32, (nh, GDN_NV), 1)
    expand = jnp.where(ec // GDN_DV == er, 1.0, 0.0).astype(F32)
    egl_ref[0] = jnp.exp(_hdot(gl, expand))


def _split_bf16(v):
    hi = v.astype(BF16)
    return hi, (v - hi.astype(F32)).astype(BF16)


def _dot_x3(ah, al, bh, bl):
    return _dot(ah, bh) + (_dot(al, bh) + _dot(ah, bl))


def _unit_lower_inverses(mats, eye, order):
    n = mats[0].shape[0]
    ts = [eye - a for a in mats]
    ps = [_dot_x3(ah, al, ah, al) for ah, al in map(_split_bf16, mats)]
    for _ in range(order.bit_length() - 3):
        tsp, psp = list(map(_split_bf16, ts)), list(map(_split_bf16, ps))
        tps = [_dot_x3(jnp.concatenate([th, ph], axis=0), jnp.concatenate([tl, pl_], axis=0), ph, pl_)
               for (th, tl), (ph, pl_) in zip(tsp, psp)]
        ts = [t + tp[:n] for t, tp in zip(ts, tps)]
        ps = [tp[n:] for tp in tps]
    tsp, psp = list(map(_split_bf16, ts)), list(map(_split_bf16, ps))
    return [t + _dot_x3(th, tl, ph, pl_) for t, (th, tl), (ph, pl_) in zip(ts, tsp, psp)]


def _gdn_pre_kernel(q_ref, k_ref, v_ref, bcol_ref, gcol_ref, rs_ref, w_ref, u_ref, qh_ref, kt_ref, in_ref):
    tc, c, nh = GDN_TC, GDN_CHUNK, GDN_V_HEADS
    j = pl.program_id(1)
    head_lane = lax.broadcasted_iota(jnp.int32, (tc, nh), 1)
    ri = lax.broadcasted_iota(jnp.int32, (tc, tc), 0)
    ci = lax.broadcasted_iota(jnp.int32, (tc, tc), 1)
    same = (ri // c) == (ci // c)
    tril = same & (ri >= ci)
    strict = same & (ri > ci)
    eye = jnp.where(ri == ci, 1.0, 0.0).astype(F32)
    lane = lax.broadcasted_iota(jnp.int32, (1, LANES), 1)

    chains = []
    for g in range(GDN_TP // tc):
        rows = slice(g * tc, (g + 1) * tc)
        q, k = q_ref[0, rows, :], k_ref[0, rows, :]
        qb, kb = q.astype(BF16), k.astype(BF16)
        kk = lax.dot_general(kb, kb, NT_DIMS, preferred_element_type=F32)
        qk = lax.dot_general(qb, kb, NT_DIMS, preferred_element_type=F32)
        for hh in range(2):
            h = 2 * j + hh
            sel = head_lane == h
            beta_c = jnp.sum(jnp.where(sel, bcol_ref[0, rows, :], 0.0), -1, keepdims=True)
            gc_c = jnp.sum(jnp.where(sel, gcol_ref[0, rows, :], 0.0), -1, keepdims=True)
            gl_c = jnp.concatenate([jnp.broadcast_to(gc_c[(m + 1) * c - 1:(m + 1) * c, :], (c, 1))
                                    for m in range(tc // c)], axis=0)
            beta_r = rs_ref[0, pl.ds(h, 1), rows]
            gc_r = rs_ref[0, pl.ds(nh + h, 1), rows]
            decay = jnp.exp(jnp.where(tril, gc_c - gc_r, -jnp.inf))
            chains.append(dict(a=jnp.where(strict, beta_c * kk * decay, 0.0), rows=rows, hh=hh, q=q, k=k, kb=kb,
                               qk=qk, decay=decay, beta_r=beta_r, gc_r=gc_r, gc_c=gc_c, gl_c=gl_c))
    inverses = _unit_lower_inverses([ch["a"] for ch in chains], eye, c)

    intra_prev = None
    for ch, tm in zip(chains, inverses):
        rows, hh = ch["rows"], ch["hh"]
        cols = slice(hh * GDN_DV, (hh + 1) * GDN_DV)
        vh = v_ref[0, rows, cols].astype(BF16)
        u_ref[0, rows, cols] = _dot((tm * ch["beta_r"]).astype(BF16), vh)
        w_ref[0, rows, cols] = _dot((tm * (ch["beta_r"] * jnp.exp(ch["gc_r"]))).astype(BF16), ch["kb"]).astype(BF16)
        qh_ref[0, rows, cols] = (ch["q"] * jnp.exp(ch["gc_c"])).astype(BF16)
        kt_ref[0, rows, cols] = (ch["k"] * jnp.exp(ch["gl_c"] - ch["gc_c"])).astype(BF16)
        intra = jnp.where(tril, ch["qk"] * ch["decay"], 0.0)
        half = intra[:, :LANES] + intra[:, LANES:]
        full = half + pltpu.roll(half, c, axis=1)
        if hh == 1:
            in_ref[0, rows, :] = jnp.where(lane < c, intra_prev, full).astype(BF16)
        intra_prev = full


def _gdn_scan_kernel(w_ref, u_ref, qh_ref, kt_ref, in_ref, egl_ref, o_ref, s_ref):
    c = GDN_CHUNK
    pw = 2 * GDN_DV
    ri = lax.broadcasted_iota(jnp.int32, (pw, pw), 0)
    ci = lax.broadcasted_iota(jnp.int32, (pw, pw), 1)
    block_diag = (ri // GDN_DK) == (ci // GDN_DV)

    @pl.when(pl.program_id(1) == 0)
    def _():
        s_ref[...] = jnp.zeros(s_ref.shape, F32)

    def step(n, carry):
        r0 = pl.multiple_of(n * c, c)
        rows = pl.ds(r0, c)
        zero = jnp.zeros((c, GDN_DV), BF16)
        pairs = range(GDN_K_HEADS)
        cols = [slice(p * pw, (p + 1) * pw) for p in pairs]
        ss = [s_ref[p] for p in pairs]
        rs = [_dot(jnp.concatenate([w_ref[0, rows, cols[p]], qh_ref[0, rows, cols[p]]], axis=0),
                   ss[p].astype(BF16)) for p in pairs]
        vns = [(u_ref[0, rows, cols[p]] - rs[p][:c]).astype(BF16) for p in pairs]
        vbds = [jnp.concatenate([jnp.concatenate([vn[:, :GDN_DV], zero], axis=1),
                                 jnp.concatenate([zero, vn[:, GDN_DV:]], axis=1)], axis=0) for vn in vns]
        outs = [rs[p][c:] + _dot(in_ref[0, rows, p * 2 * c:(p + 1) * 2 * c], vbds[p]) for p in pairs]
        upds = [lax.dot_general(kt_ref[0, rows, cols[p]], vns[p], TN_DIMS, preferred_element_type=F32)
                for p in pairs]
        o_ref[0, rows, :] = jnp.concatenate(outs, axis=1)
        s_ref[...] = jnp.stack([ss[p] * egl_ref[0, pl.ds(n, 1), cols[p]] + jnp.where(block_diag, upds[p], 0.0)
                                for p in pairs], axis=0)
        return carry

    lax.fori_loop(0, GDN_TS // c, step, 0)


def _gdn_out_kernel(o_ref, z_ref, x_ref, ng_ref, wout_ref, g_ref, b_ref, y_ref):
    parts = []
    for h in range(GDN_V_HEADS):
        cols = slice(h * GDN_DV, (h + 1) * GDN_DV)
        oh = o_ref[0, :, cols]
        oh = oh * lax.rsqrt(jnp.mean(oh * oh, -1, keepdims=True) + RMS_EPS) * ng_ref[...]
        parts.append((oh * _silu(z_ref[0, :, cols].astype(F32))).astype(BF16))
    mix = _dot(jnp.concatenate(parts, axis=1), wout_ref[...])
    y_ref[0] = _layer_norm(DN_ALPHA * x_ref[0] + mix, g_ref[...], b_ref[...])


def _gdn_ln(x, w_in, w_conv, a_log, dt_bias, norm_g, w_out, g, b):
    bn, s, d = x.shape
    nh, ts = GDN_V_HEADS, GDN_TS
    nchunks = s // GDN_CHUNK
    w_qkv = w_in[:, :GDN_QKV].astype(BF16)
    w_z = w_in[:, GDN_QKV:GDN_QKV + GDN_NV].astype(BF16)
    w_ba = w_in[:, GDN_QKV + GDN_NV:].astype(BF16)
    tile = lambda w: pl.BlockSpec((1, ts, w), lambda i, jj: (i, jj, 0))
    q, k, v, z, bcol, gcol, rs, egl = pl.pallas_call(
        _gdn_in_kernel,
        grid=(bn, s // ts),
        in_specs=[tile(d), _const_spec(w_qkv.shape), _const_spec(w_conv.shape), _const_spec(w_z.shape),
                  _const_spec(w_ba.shape), _const_spec((2 * nh, d)), _const_spec((1, nh)), _const_spec((1, nh)),
                  _const_spec((nh, 1)), _const_spec((nh, 1))],
        out_specs=[tile(GDN_NK), tile(GDN_NK), tile(GDN_NV), tile(GDN_NV), tile(nh), tile(nh),
                   pl.BlockSpec((1, 2 * nh, ts), lambda i, jj: (i, 0, jj)),
                   pl.BlockSpec((1, ts // GDN_CHUNK, GDN_NV), lambda i, jj: (i, jj, 0))],
        out_shape=[jax.ShapeDtypeStruct((bn, s, GDN_NK), F32), jax.ShapeDtypeStruct((bn, s, GDN_NK), F32),
                   jax.ShapeDtypeStruct((bn, s, GDN_NV), F32), jax.ShapeDtypeStruct((bn, s, GDN_NV), BF16),
                   jax.ShapeDtypeStruct((bn, s, nh), F32), jax.ShapeDtypeStruct((bn, s, nh), F32),
                   jax.ShapeDtypeStruct((bn, 2 * nh, s), F32),
                   jax.ShapeDtypeStruct((bn, nchunks, GDN_NV), F32)],
        scratch_shapes=[pltpu.VMEM((GDN_HALO + ts, GDN_QKV), F32)],
        compiler_params=_cparams("parallel", "arbitrary"),
        name="gdn_in",
    )(x, w_qkv, w_conv, w_z, w_ba, w_ba.T, a_log.reshape(1, nh), dt_bias.reshape(1, nh),
      a_log.reshape(nh, 1), dt_bias.reshape(nh, 1))

    tc = GDN_TP
    pair = lambda w: pl.BlockSpec((1, tc, w), lambda i, jj, tt: (i, tt, jj))
    whole = lambda w: pl.BlockSpec((1, tc, w), lambda i, jj, tt: (i, tt, 0))
    w, u, qh, kt, intra = pl.pallas_call(
        _gdn_pre_kernel,
        grid=(bn, GDN_K_HEADS, s // tc),
        in_specs=[pair(GDN_DK), pair(GDN_DK), pair(2 * GDN_DV), whole(nh), whole(nh),
                  pl.BlockSpec((1, 2 * nh, tc), lambda i, jj, tt: (i, 0, tt))],
        out_specs=[pair(2 * GDN_DK), pair(2 * GDN_DV), pair(2 * GDN_DK), pair(2 * GDN_DK), pair(2 * GDN_CHUNK)],
        out_shape=[jax.ShapeDtypeStruct((bn, s, 2 * GDN_NK), BF16), jax.ShapeDtypeStruct((bn, s, GDN_NV), F32),
                   jax.ShapeDtypeStruct((bn, s, 2 * GDN_NK), BF16), jax.ShapeDtypeStruct((bn, s, 2 * GDN_NK), BF16),
                   jax.ShapeDtypeStruct((bn, s, GDN_V_HEADS * GDN_CHUNK), BF16)],
        compiler_params=_cparams("parallel", "parallel", "parallel"),
        name="gdn_pre",
    )(q, k, v, bcol, gcol, rs)

    o = pl.pallas_call(
        _gdn_scan_kernel,
        grid=(bn, s // ts),
        in_specs=[tile(2 * GDN_NK), tile(GDN_NV), tile(2 * GDN_NK), tile(2 * GDN_NK),
                  tile(GDN_V_HEADS * GDN_CHUNK),
                  pl.BlockSpec((1, ts // GDN_CHUNK, GDN_NV), lambda i, jj: (i, jj, 0))],
        out_specs=tile(GDN_NV),
        out_shape=jax.ShapeDtypeStruct((bn, s, GDN_NV), F32),
        scratch_shapes=[pltpu.VMEM((GDN_K_HEADS, 2 * GDN_DK, 2 * GDN_DV), F32)],
        compiler_params=_cparams("parallel", "arbitrary"),
        name="gdn_scan",
    )(w, u, qh, kt, intra, egl)

    return pl.pallas_call(
        _gdn_out_kernel,
        grid=(bn, s // ts),
        in_specs=[tile(GDN_NV), tile(GDN_NV), tile(d), _const_spec((1, GDN_DV)), _const_spec(w_out.shape),
                  _const_spec((1, d)), _const_spec((1, d))],
        out_specs=tile(d),
        out_shape=jax.ShapeDtypeStruct(x.shape, F32),
        compiler_params=_cparams("parallel", "parallel"),
        name="gdn_out",
    )(o, z, x, norm_g.reshape(1, GDN_DV), w_out.astype(BF16), g.reshape(1, d), b.reshape(1, d))


ATT_HEADS = D_MODEL // 64
ATT_HEAD_DIM = 64
ROPE_DIM = ATT_HEAD_DIM // 4
ROPE_HALF = ROPE_DIM // 2
ATT_NOPE_DIM = ATT_HEAD_DIM - ROPE_DIM
Q_RANK = D_MODEL // 4
KV_RANK = D_MODEL // 8
IDX_HEADS = 8
IDX_DIM = ATT_HEAD_DIM
TOPK_MAX = 256
QK_WIDTH = 2 * LANES
DSA_TS = 512
DSA_QB = 128
DSA_KT = 512
HEAD_GROUP = 4
MASKED = -1e30
LOG2_E = 1.4426950408889634


def _rope_tab_kernel(pos_ref, inv_ref, rot_ref, sgn_ref, cos_ref, sin_ref):
    ang = pos_ref[0].astype(F32) * inv_ref[...]
    cos_ref[0] = jnp.where(rot_ref[...] > 0.0, jnp.cos(ang), 1.0)
    sin_ref[0] = jnp.sin(ang) * sgn_ref[...]


def _rope_tables(positions):
    bn, s = positions.shape
    lane = jnp.arange(LANES)
    inv = ROPE_THETA ** (-jnp.arange(0, ROPE_DIM, 2, dtype=F32) / ROPE_DIM)
    within = lane % ATT_HEAD_DIM
    rot = (within < ROPE_DIM).astype(F32)
    sgn = jnp.where(within < ROPE_HALF, -1.0, 1.0).astype(F32) * rot
    row = lambda v: v.reshape(1, LANES)
    ts = DSA_TS
    tile = pl.BlockSpec((1, ts, LANES), lambda i, j: (i, j, 0))
    return pl.pallas_call(
        _rope_tab_kernel,
        grid=(bn, s // ts),
        in_specs=[pl.BlockSpec((1, ts, 1), lambda i, j: (i, j, 0))] + [_const_spec((1, LANES))] * 3,
        out_specs=[tile, tile],
        out_shape=[jax.ShapeDtypeStruct((bn, s, LANES), F32)] * 2,
        compiler_params=_cparams("parallel", "parallel"),
        name="rope_tables",
    )(positions.reshape(bn, s, 1), row(inv[lane % ROPE_HALF]), row(rot), row(sgn))


def _rope(v, cosm, sinm):
    lane = lax.broadcasted_iota(jnp.int32, (1, LANES), 1)
    low = (lane % ROPE_DIM) < ROPE_HALF
    cols = []
    for c0 in range(0, v.shape[1], LANES):
        blk = v[:, c0:c0 + LANES]
        partner = jnp.where(low, pltpu.roll(blk, LANES - ROPE_HALF, axis=1), pltpu.roll(blk, ROPE_HALF, axis=1))
        cols.append(blk * cosm + partner * sinm)
    return cols[0] if len(cols) == 1 else jnp.concatenate(cols, axis=1)


def _rms_norm(v, g):
    return v * lax.rsqrt(jnp.mean(v * v, -1, keepdims=True) + RMS_EPS) * g


def _dsa_proj_kernel(x_ref, cos_ref, sin_ref, wdq_ref, qn_ref, wuq_ref, wcomb_ref, wdkv_ref, kvn_ref, wkr_ref,
                     wiq_ref, wik_ref, ikg_ref, ikb_ref, prot_ref, wiwt_ref,
                     q2_ref, k2_ref, qi_ref, ki_ref, wit_ref):
    xb = x_ref[0].astype(BF16)
    cosm, sinm = cos_ref[0], sin_ref[0]
    cq = _rms_norm(_dot(xb, wdq_ref[...]), qn_ref[...]).astype(BF16)
    q = _rope(_dot(cq, wuq_ref[...]), cosm, sinm) * (ATT_HEAD_DIM ** -0.5 * LOG2_E)
    qb16 = q.astype(BF16)
    for g in range(ATT_HEADS // HEAD_GROUP):
        q2_ref[0, :, g * HEAD_GROUP * QK_WIDTH:(g + 1) * HEAD_GROUP * QK_WIDTH] = _dot(
            qb16[:, g * HEAD_GROUP * ATT_HEAD_DIM:(g + 1) * HEAD_GROUP * ATT_HEAD_DIM], wcomb_ref[g]).astype(BF16)
    ckv = _rms_norm(_dot(xb, wdkv_ref[...]), kvn_ref[...])
    kr = _rope(_dot(xb, wkr_ref[...]), cosm, sinm)
    k2_ref[0] = jnp.concatenate([ckv, kr], axis=1).astype(BF16)
    qi = _rope(_dot(cq, wiq_ref[...]), cosm, sinm).astype(BF16)
    for h in range(IDX_HEADS):
        qi_ref[0, h] = qi[:, h * IDX_DIM:(h + 1) * IDX_DIM]
    kin = _layer_norm(_dot(xb, wik_ref[...]), ikg_ref[...], ikb_ref[...])
    ki = kin * cosm[:, :IDX_DIM] + _hdot(kin, prot_ref[...]) * sinm[:, :IDX_DIM]
    ki_ref[0] = ki.astype(BF16)
    wit = lax.dot_general(wiwt_ref[...], xb, NT_DIMS, preferred_element_type=F32)
    wit_ref[0] = wit * (IDX_HEADS * IDX_DIM) ** -0.5


def _dsa_attn_kernel(q2_ref, k2_ref, qi_ref, ki_ref, wit_ref, o_ref,
                     key_ref, mask_ref, q2s_ref, lga_ref, lgb_ref, pa_ref, pb_ref, acc_ref):
    qb, kt_sz = DSA_QB, DSA_KT
    i = pl.program_id(1)
    n_tiles = (i * qb) // kt_sz + 1
    q_pos = i * qb + lax.broadcasted_iota(jnp.int32, (1, qb), 1)

    def score_tile(t, carry):
        r0 = pl.multiple_of(t * kt_sz, kt_sz)
        kblk = ki_ref[0, pl.ds(r0, kt_sz), :]
        acc = jnp.zeros((kt_sz, qb), F32)
        for h in range(IDX_HEADS):
            sc = lax.dot_general(kblk, qi_ref[0, h], NT_DIMS, preferred_element_type=F32)
            acc = acc + jnp.maximum(sc, 0.0) * wit_ref[0, h:h + 1, :]
        k_pos = r0 + lax.broadcasted_iota(jnp.int32, (kt_sz, 1), 0)
        acc = jnp.where(k_pos <= q_pos, acc, -jnp.inf)
        bits = pltpu.bitcast(acc, jnp.int32)
        key_ref[pl.ds(r0, kt_sz), :] = bits ^ ((bits >> 31) & jnp.int32(0x7FFFFFFF))
        return carry

    lax.fori_loop(0, n_tiles, score_tile, 0)

    def count(pred):
        def body(t, acc):
            r0 = pl.multiple_of(t * kt_sz, kt_sz)
            k_pos = r0 + lax.broadcasted_iota(jnp.int32, (kt_sz, 1), 0)
            hit = jnp.where(pred(key_ref[pl.ds(r0, kt_sz), :], k_pos), 1, 0).astype(jnp.int32)
            return acc + jnp.sum(hit.reshape(kt_sz // SUBLANES, SUBLANES, qb), axis=0)
        acc = lax.fori_loop(0, n_tiles, body, jnp.zeros((SUBLANES, qb), jnp.int32))
        return jnp.sum(acc, axis=0, keepdims=True)

    int_min = jnp.int32(-2 ** 31)
    topk = jnp.int32(TOPK_MAX)

    def search_bit(b, thr):
        cand = thr + lax.shift_left(jnp.int32(1), jnp.int32(31) - b)
        return jnp.where(count(lambda kb, kp: kb >= cand) >= topk, cand, thr)

    few_keys = (i + 1) * qb <= TOPK_MAX
    thr0 = jnp.full((1, qb), int_min, jnp.int32)
    thr = lax.cond(few_keys, lambda: thr0, lambda: lax.fori_loop(0, 32, search_bit, thr0))

    need = topk - count(lambda kb, kp: kb > thr)
    n_eq = count(lambda kb, kp: kb == thr)
    full_cut = jnp.full((1, qb), jnp.int32(2 ** 30), jnp.int32)

    def search_cut():
        def cut_bit(b, cut):
            cand = cut + lax.shift_left(jnp.int32(1), jnp.int32(12) - b)
            below = count(lambda kb, kp: (kb == thr) & (kp < cand))
            return jnp.where(below < need, cand, cut)
        return lax.fori_loop(0, 13, cut_bit, jnp.zeros((1, qb), jnp.int32))

    tied = jnp.logical_and(jnp.logical_not(few_keys), jnp.max(jnp.where(n_eq != need, 1, 0)) > 0)
    cut = lax.cond(tied, search_cut, lambda: full_cut)

    def mask_tile(t, carry):
        r0 = pl.multiple_of(t * kt_sz, kt_sz)
        kb = key_ref[pl.ds(r0, kt_sz), :]
        k_pos = r0 + lax.broadcasted_iota(jnp.int32, (kt_sz, 1), 0)
        keep = ((kb > thr) | ((kb == thr) & (k_pos <= cut))) & (k_pos <= q_pos)
        mask_ref[:, pl.ds(r0, kt_sz)] = jnp.where(keep, 0.0, MASKED).astype(F32).T
        return carry

    lax.fori_loop(0, n_tiles, mask_tile, 0)

    nh = ATT_HEADS
    half = kt_sz // 2
    for h in range(nh):
        q2s_ref[h * qb:(h + 1) * qb, :] = q2_ref[0, :, h * QK_WIDTH:(h + 1) * QK_WIDTH]
    acc_ref[...] = jnp.zeros(acc_ref.shape, F32)
    vlane = lax.broadcasted_iota(jnp.int32, (1, QK_WIDTH), 1)
    ones_col = jnp.where(vlane == KV_RANK, 1.0, 0.0).astype(BF16)

    def logits_into(dst_ref, key0):
        dst_ref[...] = lax.dot_general(q2s_ref[...], k2_ref[0, pl.ds(pl.multiple_of(key0, half), half), :],
                                       NT_DIMS, preferred_element_type=F32)

    def softmax_pv(src_ref, prob_ref, key0, m):
        key0 = pl.multiple_of(key0, half)
        mask = mask_ref[:, pl.ds(key0, half)]
        wide = lambda v: jnp.concatenate([v] * (half // LANES), axis=1)
        m_parts, alpha_parts = [], []
        for h in range(nh):
            rows = slice(h * qb, (h + 1) * qb)
            lg = src_ref[rows, :] + mask
            m_new = jnp.maximum(m[rows], jnp.max(lg, -1, keepdims=True))
            prob_ref[rows, :] = jnp.exp2(lg - wide(m_new)).astype(BF16)
            alpha_parts.append(jnp.exp2(m[rows] - m_new))
            m_parts.append(m_new)
        v1 = jnp.where(vlane < KV_RANK, k2_ref[0, pl.ds(key0, half), :], ones_col)
        alpha = jnp.concatenate(alpha_parts, axis=0)
        acc_ref[...] = acc_ref[...] * jnp.concatenate([alpha] * (QK_WIDTH // LANES), axis=1) + _dot(prob_ref[...], v1)
        return jnp.concatenate(m_parts, axis=0)

    logits_into(lga_ref, 0)
    last_key0 = (n_tiles - 1) * kt_sz

    def attn_tile(t, m):
        key0 = t * kt_sz
        logits_into(lgb_ref, key0 + half)
        m = softmax_pv(lga_ref, pa_ref, key0, m)
        logits_into(lga_ref, jnp.minimum(key0 + kt_sz, last_key0))
        return softmax_pv(lgb_ref, pb_ref, key0 + half, m)

    lax.fori_loop(0, n_tiles, attn_tile, jnp.full((nh * qb, LANES), MASKED, F32))
    for h in range(nh):
        acc = acc_ref[h * qb:(h + 1) * qb, :]
        o_ref[0, :, h * KV_RANK:(h + 1) * KV_RANK] = (
            acc[:, :KV_RANK] / acc[:, KV_RANK:KV_RANK + 1]).astype(o_ref.dtype)


def _dsa_out_kernel(o_ref, x_ref, wuv_ref, wo_ref, g_ref, b_ref, y_ref):
    o = jnp.concatenate(
        [_dot(o_ref[0, :, g * HEAD_GROUP * KV_RANK:(g + 1) * HEAD_GROUP * KV_RANK], wuv_ref[g])
         for g in range(ATT_HEADS // HEAD_GROUP)], axis=1).astype(BF16)
    mix = _dot(o, wo_ref[...])
    y_ref[0] = _layer_norm(DN_ALPHA * x_ref[0] + mix, g_ref[...], b_ref[...])


def _block_diag(blocks):
    n, r, c = blocks.shape
    idx = jnp.arange(n)
    return jnp.zeros((n, r, n, c), blocks.dtype).at[idx, :, idx, :].set(blocks).reshape(n * r, n * c)


def _dsa_ln(x, positions, w_dq, q_norm, w_uq, w_dkv, kv_norm, w_kr, w_uk, w_uv, w_o,
            w_iq, w_ik, ik_g, ik_b, w_iw, g, b):
    bn, s, d = x.shape
    ts, nh = DSA_TS, ATT_HEADS
    cosm, sinm = _rope_tables(positions)

    head_map = jnp.zeros((nh, ATT_HEAD_DIM, QK_WIDTH), F32)
    head_map = head_map.at[:, ROPE_DIM:, :KV_RANK].set(w_uk)
    head_map = head_map.at[:, :ROPE_DIM, KV_RANK:KV_RANK + ROPE_DIM].set(jnp.eye(ROPE_DIM, dtype=F32))
    groups = range(0, nh, HEAD_GROUP)
    w_comb = jnp.stack([_block_diag(head_map[h:h + HEAD_GROUP]) for h in groups]).astype(BF16)
    w_kr_pad = jnp.zeros((d, LANES), F32).at[:, :ROPE_DIM].set(w_kr).astype(BF16)
    lane = jnp.arange(IDX_DIM)
    partner = jnp.where(lane < ROPE_HALF, lane + ROPE_HALF, lane - ROPE_HALF)
    p_rot = ((lane[:, None] == partner[None, :]) & (lane[None, :] < ROPE_DIM)).astype(F32)

    tile = lambda w: pl.BlockSpec((1, ts, w), lambda i, j: (i, j, 0))
    consts = [w_dq.astype(BF16), q_norm.reshape(1, -1), w_uq.astype(BF16), w_comb, w_dkv.astype(BF16),
              kv_norm.reshape(1, -1), w_kr_pad, w_iq.astype(BF16), w_ik.astype(BF16), ik_g.reshape(1, -1),
              ik_b.reshape(1, -1), p_rot, w_iw.T.astype(BF16)]
    q2, k2, qi, ki, wit = pl.pallas_call(
        _dsa_proj_kernel,
        grid=(bn, s // ts),
        in_specs=[tile(d), tile(LANES), tile(LANES)] + [_const_spec(c.shape) for c in consts],
        out_specs=[tile(nh * QK_WIDTH), tile(QK_WIDTH),
                   pl.BlockSpec((1, IDX_HEADS, ts, IDX_DIM), lambda i, j: (i, 0, j, 0)),
                   tile(IDX_DIM), pl.BlockSpec((1, IDX_HEADS, ts), lambda i, j: (i, 0, j))],
        out_shape=[jax.ShapeDtypeStruct((bn, s, nh * QK_WIDTH), BF16), jax.ShapeDtypeStruct((bn, s, QK_WIDTH), BF16),
                   jax.ShapeDtypeStruct((bn, IDX_HEADS, s, IDX_DIM), BF16),
                   jax.ShapeDtypeStruct((bn, s, IDX_DIM), BF16), jax.ShapeDtypeStruct((bn, IDX_HEADS, s), F32)],
        compiler_params=_cparams("parallel", "parallel"),
        name="dsa_proj",
    )(x, cosm, sinm, *consts)

    qb = DSA_QB
    o_lat = pl.pallas_call(
        _dsa_attn_kernel,
        grid=(bn, s // qb),
        in_specs=[pl.BlockSpec((1, qb, nh * QK_WIDTH), lambda i, j: (i, j, 0)),
                  pl.BlockSpec((1, s, QK_WIDTH), lambda i, j: (i, 0, 0)),
                  pl.BlockSpec((1, IDX_HEADS, qb, IDX_DIM), lambda i, j: (i, 0, j, 0)),
                  pl.BlockSpec((1, s, IDX_DIM), lambda i, j: (i, 0, 0)),
                  pl.BlockSpec((1, IDX_HEADS, qb), lambda i, j: (i, 0, j))],
        out_specs=pl.BlockSpec((1, qb, nh * KV_RANK), lambda i, j: (i, j, 0)),
        out_shape=jax.ShapeDtypeStruct((bn, s, nh * KV_RANK), BF16),
        scratch_shapes=[pltpu.VMEM((s, qb), jnp.int32), pltpu.VMEM((qb, s), F32),
                        pltpu.VMEM((nh * qb, QK_WIDTH), BF16),
                        pltpu.VMEM((nh * qb, DSA_KT // 2), F32), pltpu.VMEM((nh * qb, DSA_KT // 2), F32),
                        pltpu.VMEM((nh * qb, DSA_KT // 2), BF16), pltpu.VMEM((nh * qb, DSA_KT // 2), BF16),
                        pltpu.VMEM((nh * qb, QK_WIDTH), F32)],
        compiler_params=_cparams("parallel", "parallel"),
        name="dsa_attn",
    )(q2, k2, qi, ki, wit)

    w_uv_bd = jnp.stack([_block_diag(w_uv[h:h + HEAD_GROUP]) for h in groups]).astype(BF16)
    return pl.pallas_call(
        _dsa_out_kernel,
        grid=(bn, s // ts),
        in_specs=[tile(nh * KV_RANK), tile(d), _const_spec(w_uv_bd.shape), _const_spec(w_o.shape),
                  _const_spec((1, d)), _const_spec((1, d))],
        out_specs=tile(d),
        out_shape=jax.ShapeDtypeStruct(x.shape, F32),
        compiler_params=_cparams("parallel", "parallel"),
        name="dsa_out",
    )(o_lat, x, w_uv_bd, w_o.astype(BF16), g.reshape(1, d), b.reshape(1, d))


def kernel(x, positions, ln_g, ln_b, ffn_w_gate, ffn_w_up, ffn_w_down,
           conv_w_in, conv_w_dw, conv_ln_g, conv_ln_b, conv_w_out,
           sc_w_in, sc_w_conv, sc_w_out,
           dsa_w_dq, dsa_q_norm, dsa_w_uq, dsa_w_dkv, dsa_kv_norm, dsa_w_kr,
           dsa_w_uk, dsa_w_uv, dsa_w_o, dsa_w_iq, dsa_w_ik, dsa_ik_ln_g, dsa_ik_ln_b, dsa_w_iw,
           gdn_w_in, gdn_w_conv, gdn_a_log, gdn_dt_bias, gdn_norm_g, gdn_w_out):
    bn, s, d = x.shape

    def ffn(h, i, half):
        return _ffn_ln(h.reshape(bn * s, d), ffn_w_gate[i, half], ffn_w_up[i, half], ffn_w_down[i, half],
                       ln_g[i, 2 * half], ln_b[i, 2 * half]).reshape(bn, s, d)

    for i in range(DEPTH):
        m, j = i % N_MIXERS, i // N_MIXERS
        x = ffn(x, i, 0)
        g, b = ln_g[i, 1], ln_b[i, 1]
        if m == 0:
            x = _conformer_ln(x, conv_w_in[j], conv_w_dw[j], conv_ln_g[j], conv_ln_b[j], conv_w_out[j], g, b)
        elif m == 1:
            x = _sconv_ln(x, sc_w_in[j], sc_w_conv[j], sc_w_out[j], g, b)
        elif m == 2:
            x = _dsa_ln(x, positions, dsa_w_dq[j], dsa_q_norm[j], dsa_w_uq[j], dsa_w_dkv[j], dsa_kv_norm[j],
                        dsa_w_kr[j], dsa_w_uk[j], dsa_w_uv[j], dsa_w_o[j], dsa_w_iq[j], dsa_w_ik[j],
                        dsa_ik_ln_g[j], dsa_ik_ln_b[j], dsa_w_iw[j], g, b)
        else:
            x = _gdn_ln(x, gdn_w_in[j], gdn_w_conv[j], gdn_a_log[j], gdn_dt_bias[j], gdn_norm_g[j],
                        gdn_w_out[j], g, b)
        x = ffn(x, i, 1)
    return x
```

```python
import functools

import jax
import jax.numpy as jnp
from jax import lax
from jax.experimental import pallas as pl
from jax.experimental.pallas import tpu as pltpu

D_MODEL = 1024
DEPTH = 4
N_MIXERS = 4
FFN_DIM = ((8 * D_MODEL // 3 + 127) // 128) * 128
DN_ALPHA = (2.0 * DEPTH) ** 0.25
ROPE_THETA = 500000.0
LN_EPS = 1e-5
RMS_EPS = 1e-6
CONF_KERNEL = 31
SCONV_KERNEL = 3

BF16 = jnp.bfloat16
F32 = jnp.float32

V7X_VMEM_LIMIT_BYTES = 56 * 1024 * 1024
SUBLANES = 8
LANES = 128


def _cparams(*sem):
    return pltpu.CompilerParams(dimension_semantics=sem, vmem_limit_bytes=V7X_VMEM_LIMIT_BYTES)


def _layer_norm(y, g, b):
    mu = jnp.mean(y, -1, keepdims=True)
    yc = y - mu
    var = jnp.mean(yc * yc, -1, keepdims=True)
    return yc * lax.rsqrt(var + LN_EPS) * g + b


def _silu(h):
    return h * jax.nn.sigmoid(h)


def _dot(a, b):
    return jnp.dot(a, b, preferred_element_type=F32)


def _const_spec(shape):
    nd = len(shape)
    return pl.BlockSpec(shape, lambda *_: (0,) * nd, pipeline_mode=pl.Buffered(1))


FFN_TM = 512
FFN_CHUNK = 768


def _ffn_kernel(x_ref, wg_ref, wu_ref, wd_ref, g_ref, b_ref, o_ref):
    x = x_ref[...]
    xb = x.astype(BF16)
    bounds = list(range(0, FFN_DIM, FFN_CHUNK)) + [FFN_DIM]
    gate_up = lambda c: (_dot(xb, wg_ref[:, bounds[c]:bounds[c + 1]]), _dot(xb, wu_ref[:, bounds[c]:bounds[c + 1]]))
    acc = jnp.zeros(x.shape, F32)
    h, u = gate_up(0)
    for c in range(len(bounds) - 1):
        nxt = gate_up(c + 1) if c + 2 < len(bounds) else None
        a = (_silu(h) * u).astype(BF16)
        acc = acc + _dot(a, wd_ref[bounds[c]:bounds[c + 1], :])
        if nxt is not None:
            h, u = nxt
    o_ref[...] = _layer_norm(DN_ALPHA * x + 0.5 * acc, g_ref[...], b_ref[...])


def _ffn_ln(x2, wg, wu, wd, g, b):
    t, d = x2.shape
    row = pl.BlockSpec((FFN_TM, d), lambda i: (i, 0))
    return pl.pallas_call(
        _ffn_kernel,
        grid=(t // FFN_TM,),
        in_specs=[row, _const_spec(wg.shape), _const_spec(wu.shape), _const_spec(wd.shape),
                  _const_spec((1, d)), _const_spec((1, d))],
        out_specs=row,
        out_shape=jax.ShapeDtypeStruct((t, d), F32),
        compiler_params=_cparams("parallel"),
        name="ffn_ln",
    )(x2, wg.astype(BF16), wu.astype(BF16), wd.astype(BF16), g.reshape(1, d), b.reshape(1, d))


def _causal_dwconv(buf_ref, u, w_ref, width, halo, ts, first_tile):
    c = u.shape[-1]

    @pl.when(first_tile)
    def _():
        buf_ref[0:halo, :] = jnp.zeros((halo, c), F32)

    @pl.when(jnp.logical_not(first_tile))
    def _():
        buf_ref[0:halo, :] = buf_ref[ts:ts + halo, :]

    buf_ref[halo:halo + ts, :] = u
    off = halo - (width - 1)
    acc = w_ref[width - 1:width, :] * u
    for k in range(width - 1):
        acc = acc + w_ref[k:k + 1, :] * buf_ref[off + k:off + k + ts, :]
    return acc


def _causal_dwconv_wide(buf_ref, sh_ref, out_ref, u, w_ref, width, halo, ts, first_tile):
    c = u.shape[-1]

    @pl.when(first_tile)
    def _():
        buf_ref[0:halo, :] = jnp.zeros((halo, c), F32)

    @pl.when(jnp.logical_not(first_tile))
    def _():
        buf_ref[0:halo, :] = buf_ref[ts:ts + halo, :]

    buf_ref[halo:halo + ts, :] = u
    off = halo - (width - 1)
    n = halo + ts - SUBLANES
    for b in range(1, SUBLANES):
        sh_ref[b - 1, 0:n, :] = buf_ref[b:b + n, :]

    def block(i, carry):
        r = pl.multiple_of(i * DWCONV_ROWS, DWCONV_ROWS)
        for c0 in range(0, c, DWCONV_COLS):
            cols = slice(c0, c0 + DWCONV_COLS)
            acc = w_ref[width - 1:width, cols] * buf_ref[pl.ds(halo + r, DWCONV_ROWS), cols]
            for k in range(width - 1):
                a, b = divmod(off + k, SUBLANES)
                src = buf_ref if b == 0 else sh_ref.at[b - 1]
                acc = acc + w_ref[k:k + 1, cols] * src[pl.ds(a * SUBLANES + r, DWCONV_ROWS), cols]
            out_ref[pl.ds(r, DWCONV_ROWS), cols] = acc
        return carry

    lax.fori_loop(0, ts // DWCONV_ROWS, block, 0)
    return out_ref[...]


CONV_TS = 512
CONF_HALO = 32
DWCONV_ROWS = 64
DWCONV_COLS = 256


def _conf_kernel(x_ref, win_ref, wdw_ref, lng_ref, lnb_ref, wout_ref, g_ref, b_ref, o_ref, buf_ref, sh_ref, cv_ref):
    d = D_MODEL
    x = x_ref[0]
    uu = _dot(x.astype(BF16), win_ref[...])
    u = uu[:, :d] * jax.nn.sigmoid(uu[:, d:])
    cv = _causal_dwconv_wide(buf_ref, sh_ref, cv_ref, u, wdw_ref, CONF_KERNEL, CONF_HALO, CONV_TS,
                             pl.program_id(1) == 0)
    v = _silu(_layer_norm(cv, lng_ref[...], lnb_ref[...]))
    mix = _dot(v.astype(BF16), wout_ref[...])
    o_ref[0] = _layer_norm(DN_ALPHA * x + mix, g_ref[...], b_ref[...])


def _conformer_ln(x, w_in, w_dw, ln_g, ln_b, w_out, g, b):
    bn, s, d = x.shape
    tile = pl.BlockSpec((1, CONV_TS, d), lambda i, j: (i, j, 0))
    return pl.pallas_call(
        _conf_kernel,
        grid=(bn, s // CONV_TS),
        in_specs=[tile, _const_spec(w_in.shape), _const_spec(w_dw.shape), _const_spec((1, d)),
                  _const_spec((1, d)), _const_spec(w_out.shape), _const_spec((1, d)), _const_spec((1, d))],
        out_specs=tile,
        out_shape=jax.ShapeDtypeStruct(x.shape, F32),
        scratch_shapes=[pltpu.VMEM((CONF_HALO + CONV_TS, d), F32),
                        pltpu.VMEM((SUBLANES - 1, CONF_HALO + CONV_TS, d), F32), pltpu.VMEM((CONV_TS, d), F32)],
        compiler_params=_cparams("parallel", "arbitrary"),
        name="conformer_ln",
    )(x, w_in.astype(BF16), w_dw, ln_g.reshape(1, d), ln_b.reshape(1, d), w_out.astype(BF16),
      g.reshape(1, d), b.reshape(1, d))


SCONV_HALO = SUBLANES


def _sconv_kernel(x_ref, win_ref, wc_ref, wout_ref, g_ref, b_ref, o_ref, buf_ref):
    d = D_MODEL
    x = x_ref[0]
    bch = _dot(x.astype(BF16), win_ref[...])
    cv = _causal_dwconv(buf_ref, bch[:, d:2 * d] * bch[:, 2 * d:], wc_ref, SCONV_KERNEL, SCONV_HALO,
                        CONV_TS, pl.program_id(1) == 0)
    y = bch[:, :d] * cv
    mix = _dot(y.astype(BF16), wout_ref[...])
    o_ref[0] = _layer_norm(DN_ALPHA * x + mix, g_ref[...], b_ref[...])


def _sconv_ln(x, w_in, w_conv, w_out, g, b):
    bn, s, d = x.shape
    tile = pl.BlockSpec((1, CONV_TS, d), lambda i, j: (i, j, 0))
    return pl.pallas_call(
        _sconv_kernel,
        grid=(bn, s // CONV_TS),
        in_specs=[tile, _const_spec(w_in.shape), _const_spec(w_conv.shape), _const_spec(w_out.shape),
                  _const_spec((1, d)), _const_spec((1, d))],
        out_specs=tile,
        out_shape=jax.ShapeDtypeStruct(x.shape, F32),
        scratch_shapes=[pltpu.VMEM((SCONV_HALO + CONV_TS, d), F32)],
        compiler_params=_cparams("parallel", "arbitrary"),
        name="sconv_ln",
    )(x, w_in.astype(BF16), w_conv, w_out.astype(BF16), g.reshape(1, d), b.reshape(1, d))


GDN_K_HEADS = D_MODEL // 128
GDN_V_HEADS = 2 * GDN_K_HEADS
GDN_DK = 128
GDN_DV = 128
GDN_CONV = 4
GDN_CHUNK = 64
GDN_NK = GDN_K_HEADS * GDN_DK
GDN_NV = GDN_V_HEADS * GDN_DV
GDN_QKV = 2 * GDN_NK + GDN_NV
GDN_TS = 512
GDN_TC = 256
GDN_TP = 512
GDN_HALO = SUBLANES
HIGHEST = lax.Precision.HIGHEST
NT_DIMS = (((1,), (1,)), ((), ()))
TN_DIMS = (((0,), (0,)), ((), ()))


def _hdot(a, b):
    return jnp.dot(a, b, preferred_element_type=F32, precision=HIGHEST)


def _softplus(v):
    return jnp.maximum(v, 0.0) + jnp.log1p(jnp.exp(-jnp.abs(v)))


def _gdn_in_kernel(x_ref, wqkv_ref, wconv_ref, wz_ref, wba_ref, wbat_ref, alog_r_ref, dt_r_ref, alog_c_ref,
                   dt_c_ref, q_ref, k_ref, v_ref, z_ref, bcol_ref, gcol_ref, rs_ref, egl_ref, buf_ref):
    ts = GDN_TS
    xb = x_ref[0].astype(BF16)
    first = pl.program_id(1) == 0

    def conv_silu(c0, c1):
        return _silu(_causal_dwconv(buf_ref.at[:, c0:c1], _dot(xb, wqkv_ref[:, c0:c1]), wconv_ref.at[:, c0:c1],
                                    GDN_CONV, GDN_HALO, ts, first))

    for part, (o_ref, scale) in enumerate(((q_ref, GDN_DK ** -0.5), (k_ref, 1.0))):
        qk = conv_silu(part * GDN_NK, (part + 1) * GDN_NK)
        for h in range(GDN_K_HEADS):
            hd = qk[:, h * GDN_DK:(h + 1) * GDN_DK]
            o_ref[0, :, h * GDN_DK:(h + 1) * GDN_DK] = (
                hd * (lax.rsqrt(jnp.sum(hd * hd, -1, keepdims=True) + RMS_EPS) * scale))
    for part in range(GDN_NV // GDN_NK):
        v_ref[0, :, part * GDN_NK:(part + 1) * GDN_NK] = conv_silu((2 + part) * GDN_NK, (3 + part) * GDN_NK)
    z_ref[0] = _dot(xb, wz_ref[...]).astype(z_ref.dtype)

    nh = GDN_V_HEADS
    ba = _dot(xb, wba_ref[...])
    beta_c = jax.nn.sigmoid(ba[:, :nh])
    g_c = -jnp.exp(alog_r_ref[...]) * _softplus(ba[:, nh:] + dt_r_ref[...])
    bat = lax.dot_general(wbat_ref[...], xb, NT_DIMS, preferred_element_type=F32)
    beta_r = jax.nn.sigmoid(bat[:nh])
    g_r = -jnp.exp(alog_c_ref[...]) * _softplus(bat[nh:] + dt_c_ref[...])
    ri = lax.broadcasted_iota(jnp.int32, (ts, ts), 0)
    ci = lax.broadcasted_iota(jnp.int32, (ts, ts), 1)
    same = (ri // GDN_CHUNK) == (ci // GDN_CHUNK)
    low = jnp.where(same & (ci <= ri), 1.0, 0.0).astype(F32)
    upp = jnp.where(same & (ri <= ci), 1.0, 0.0).astype(F32)
    gc_c = _hdot(low, g_c)
    gc_r = _hdot(g_r, upp)
    bcol_ref[0] = beta_c
    gcol_ref[0] = gc_c
    rs_ref[0, 0:nh, :] = beta_r
    rs_ref[0, nh:2 * nh, :] = gc_r
    nchunk = ts // GDN_CHUNK
    cr = lax.broadcasted_iota(jnp.int32, (nchunk, ts), 0)
    ct = lax.broadcasted_iota(jnp.int32, (nchunk, ts), 1)
    chunk_sum = jnp.where(ct // GDN_CHUNK == cr, 1.0, 0.0).astype(F32)
    gl = _hdot(chunk_sum, g_c)
    er = lax.broadcasted_iota(jnp.int32, (nh, GDN_NV), 0)
    ec = lax.broadcasted_iota(jnp.int32, (nh, GDN_NV), 1)
    expand = jnp.where(ec // GDN_DV == er, 1.0, 0.0).astype(F32)
    egl_ref[0] = jnp.exp(_hdot(gl, expand))


def _unit_lower_inverses(mats, eye, order):
    n = mats[0].shape[0]
    ts = [eye - a for a in mats]
    ps = [_dot(ab, ab) for ab in (a.astype(BF16) for a in mats)]
    for _ in range(order.bit_length() - 3):
        pbs = [p.astype(BF16) for p in ps]
        tps = [_dot(jnp.concatenate([t.astype(BF16), pb], axis=0), pb) for t, pb in zip(ts, pbs)]
        ts = [t + tp[:n] for t, tp in zip(ts, tps)]
        ps = [tp[n:] for tp in tps]
    return [t + _dot(t.astype(BF16), p.astype(BF16)) for t, p in zip(ts, ps)]


def _gdn_pre_kernel(q_ref, k_ref, v_ref, bcol_ref, gcol_ref, rs_ref, w_ref, u_ref, qh_ref, kt_ref, in_ref):
    tc, c, nh = GDN_TC, GDN_CHUNK, GDN_V_HEADS
    j = pl.program_id(1)
    head_lane = lax.broadcasted_iota(jnp.int32, (tc, nh), 1)
    ri = lax.broadcasted_iota(jnp.int32, (tc, tc), 0)
    ci = lax.broadcasted_iota(jnp.int32, (tc, tc), 1)
    same = (ri // c) == (ci // c)
    tril = same & (ri >= ci)
    strict = same & (ri > ci)
    eye = jnp.where(ri == ci, 1.0, 0.0).astype(F32)
    lane = lax.broadcasted_iota(jnp.int32, (1, LANES), 1)

    chains = []
    for g in range(GDN_TP // tc):
        rows = slice(g * tc, (g + 1) * tc)
        q, k = q_ref[0, rows, :], k_ref[0, rows, :]
        qb, kb = q.astype(BF16), k.astype(BF16)
        kk = lax.dot_general(kb, kb, NT_DIMS, preferred_element_type=F32)
        qk = lax.dot_general(qb, kb, NT_DIMS, preferred_element_type=F32)
        for hh in range(2):
            h = 2 * j + hh
            sel = head_lane == h
            beta_c = jnp.sum(jnp.where(sel, bcol_ref[0, rows, :], 0.0), -1, keepdims=True)
            gc_c = jnp.sum(jnp.where(sel, gcol_ref[0, rows, :], 0.0), -1, keepdims=True)
            gl_c = jnp.concatenate([jnp.broadcast_to(gc_c[(m + 1) * c - 1:(m + 1) * c, :], (c, 1))
                                    for m in range(tc // c)], axis=0)
            beta_r = rs_ref[0, pl.ds(h, 1), rows]
            gc_r = rs_ref[0, pl.ds(nh + h, 1), rows]
            decay = jnp.exp(jnp.where(tril, gc_c - gc_r, -jnp.inf))
            chains.append(dict(a=jnp.where(strict, beta_c * kk * decay, 0.0), rows=rows, hh=hh, q=q, k=k, kb=kb,
                               qk=qk, decay=decay, beta_r=beta_r, gc_r=gc_r, gc_c=gc_c, gl_c=gl_c))
    inverses = _unit_lower_inverses([ch["a"] for ch in chains], eye, c)

    intra_prev = None
    for ch, tm in zip(chains, inverses):
        rows, hh = ch["rows"], ch["hh"]
        cols = slice(hh * GDN_DV, (hh + 1) * GDN_DV)
        vh = v_ref[0, rows, cols].astype(BF16)
        u_ref[0, rows, cols] = _dot((tm * ch["beta_r"]).astype(BF16), vh)
        w_ref[0, rows, cols] = _dot((tm * (ch["beta_r"] * jnp.exp(ch["gc_r"]))).astype(BF16), ch["kb"]).astype(BF16)
        qh_ref[0, rows, cols] = (ch["q"] * jnp.exp(ch["gc_c"])).astype(BF16)
        kt_ref[0, rows, cols] = (ch["k"] * jnp.exp(ch["gl_c"] - ch["gc_c"])).astype(BF16)
        intra = jnp.where(tril, ch["qk"] * ch["decay"], 0.0)
        half = intra[:, :LANES] + intra[:, LANES:]
        full = half + pltpu.roll(half, c, axis=1)
        if hh == 1:
            in_ref[0, rows, :] = jnp.where(lane < c, intra_prev, full).astype(BF16)
        intra_prev = full


def _gdn_scan_kernel(w_ref, u_ref, qh_ref, kt_ref, in_ref, egl_ref, o_ref, s_ref):
    c = GDN_CHUNK
    pw = 2 * GDN_DV
    ri = lax.broadcasted_iota(jnp.int32, (pw, pw), 0)
    ci = lax.broadcasted_iota(jnp.int32, (pw, pw), 1)
    block_diag = (ri // GDN_DK) == (ci // GDN_DV)

    @pl.when(pl.program_id(1) == 0)
    def _():
        s_ref[...] = jnp.zeros(s_ref.shape, F32)

    def step(n, carry):
        r0 = pl.multiple_of(n * c, c)
        rows = pl.ds(r0, c)
        zero = jnp.zeros((c, GDN_DV), BF16)
        pairs = range(GDN_K_HEADS)
        cols = [slice(p * pw, (p + 1) * pw) for p in pairs]
        ss = [s_ref[p] for p in pairs]
        rs = [_dot(jnp.concatenate([w_ref[0, rows, cols[p]], qh_ref[0, rows, cols[p]]], axis=0),
                   ss[p].astype(BF16)) for p in pairs]
        vns = [(u_ref[0, rows, cols[p]] - rs[p][:c]).astype(BF16) for p in pairs]
        vbds = [jnp.concatenate([jnp.concatenate([vn[:, :GDN_DV], zero], axis=1),
                                 jnp.concatenate([zero, vn[:, GDN_DV:]], axis=1)], axis=0) for vn in vns]
        outs = [rs[p][c:] + _dot(in_ref[0, rows, p * 2 * c:(p + 1) * 2 * c], vbds[p]) for p in pairs]
        upds = [lax.dot_general(kt_ref[0, rows, cols[p]], vns[p], TN_DIMS, preferred_element_type=F32)
                for p in pairs]
        o_ref[0, rows, :] = jnp.concatenate(outs, axis=1)
        s_ref[...] = jnp.stack([ss[p] * egl_ref[0, pl.ds(n, 1), cols[p]] + jnp.where(block_diag, upds[p], 0.0)
                                for p in pairs], axis=0)
        return carry

    lax.fori_loop(0, GDN_TS // c, step, 0)


def _gdn_out_kernel(o_ref, z_ref, x_ref, ng_ref, wout_ref, g_ref, b_ref, y_ref):
    parts = []
    for h in range(GDN_V_HEADS):
        cols = slice(h * GDN_DV, (h + 1) * GDN_DV)
        oh = o_ref[0, :, cols]
        oh = oh * lax.rsqrt(jnp.mean(oh * oh, -1, keepdims=True) + RMS_EPS) * ng_ref[...]
        parts.append((oh * _silu(z_ref[0, :, cols].astype(F32))).astype(BF16))
    mix = _dot(jnp.concatenate(parts, axis=1), wout_ref[...])
    y_ref[0] = _layer_norm(DN_ALPHA * x_ref[0] + mix, g_ref[...], b_ref[...])


def _gdn_ln(x, w_in, w_conv, a_log, dt_bias, norm_g, w_out, g, b):
    bn, s, d = x.shape
    nh, ts = GDN_V_HEADS, GDN_TS
    nchunks = s // GDN_CHUNK
    w_qkv = w_in[:, :GDN_QKV].astype(BF16)
    w_z = w_in[:, GDN_QKV:GDN_QKV + GDN_NV].astype(BF16)
    w_ba = w_in[:, GDN_QKV + GDN_NV:].astype(BF16)
    tile = lambda w: pl.BlockSpec((1, ts, w), lambda i, jj: (i, jj, 0))
    q, k, v, z, bcol, gcol, rs, egl = pl.pallas_call(
        _gdn_in_kernel,
        grid=(bn, s // ts),
        in_specs=[tile(d), _const_spec(w_qkv.shape), _const_spec(w_conv.shape), _const_spec(w_z.shape),
                  _const_spec(w_ba.shape), _const_spec((2 * nh, d)), _const_spec((1, nh)), _const_spec((1, nh)),
                  _const_spec((nh, 1)), _const_spec((nh, 1))],
        out_specs=[tile(GDN_NK), tile(GDN_NK), tile(GDN_NV), tile(GDN_NV), tile(nh), tile(nh),
                   pl.BlockSpec((1, 2 * nh, ts), lambda i, jj: (i, 0, jj)),
                   pl.BlockSpec((1, ts // GDN_CHUNK, GDN_NV), lambda i, jj: (i, jj, 0))],
        out_shape=[jax.ShapeDtypeStruct((bn, s, GDN_NK), F32), jax.ShapeDtypeStruct((bn, s, GDN_NK), F32),
                   jax.ShapeDtypeStruct((bn, s, GDN_NV), F32), jax.ShapeDtypeStruct((bn, s, GDN_NV), BF16),
                   jax.ShapeDtypeStruct((bn, s, nh), F32), jax.ShapeDtypeStruct((bn, s, nh), F32),
                   jax.ShapeDtypeStruct((bn, 2 * nh, s), F32),
                   jax.ShapeDtypeStruct((bn, nchunks, GDN_NV), F32)],
        scratch_shapes=[pltpu.VMEM((GDN_HALO + ts, GDN_QKV), F32)],
        compiler_params=_cparams("parallel", "arbitrary"),
        name="gdn_in",
    )(x, w_qkv, w_conv, w_z, w_ba, w_ba.T, a_log.reshape(1, nh), dt_bias.reshape(1, nh),
      a_log.reshape(nh, 1), dt_bias.reshape(nh, 1))

    tc = GDN_TP
    pair = lambda w: pl.BlockSpec((1, tc, w), lambda i, jj, tt: (i, tt, jj))
    whole = lambda w: pl.BlockSpec((1, tc, w), lambda i, jj, tt: (i, tt, 0))
    w, u, qh, kt, intra = pl.pallas_call(
        _gdn_pre_kernel,
        grid=(bn, GDN_K_HEADS, s // tc),
        in_specs=[pair(GDN_DK), pair(GDN_DK), pair(2 * GDN_DV), whole(nh), whole(nh),
                  pl.BlockSpec((1, 2 * nh, tc), lambda i, jj, tt: (i, 0, tt))],
        out_specs=[pair(2 * GDN_DK), pair(2 * GDN_DV), pair(2 * GDN_DK), pair(2 * GDN_DK), pair(2 * GDN_CHUNK)],
        out_shape=[jax.ShapeDtypeStruct((bn, s, 2 * GDN_NK), BF16), jax.ShapeDtypeStruct((bn, s, GDN_NV), F32),
                   jax.ShapeDtypeStruct((bn, s, 2 * GDN_NK), BF16), jax.ShapeDtypeStruct((bn, s, 2 * GDN_NK), BF16),
                   jax.ShapeDtypeStruct((bn, s, GDN_V_HEADS * GDN_CHUNK), BF16)],
        compiler_params=_cparams("parallel", "parallel", "parallel"),
        name="gdn_pre",
    )(q, k, v, bcol, gcol, rs)

    o = pl.pallas_call(
        _gdn_scan_kernel,
        grid=(bn, s // ts),
        in_specs=[tile(2 * GDN_NK), tile(GDN_NV), tile(2 * GDN_NK), tile(2 * GDN_NK),
                  tile(GDN_V_HEADS * GDN_CHUNK),
                  pl.BlockSpec((1, ts // GDN_CHUNK, GDN_NV), lambda i, jj: (i, jj, 0))],
        out_specs=tile(GDN_NV),
        out_shape=jax.ShapeDtypeStruct((bn, s, GDN_NV), F32),
        scratch_shapes=[pltpu.VMEM((GDN_K_HEADS, 2 * GDN_DK, 2 * GDN_DV), F32)],
        compiler_params=_cparams("parallel", "arbitrary"),
        name="gdn_scan",
    )(w, u, qh, kt, intra, egl)

    return pl.pallas_call(
        _gdn_out_kernel,
        grid=(bn, s // ts),
        in_specs=[tile(GDN_NV), tile(GDN_NV), tile(d), _const_spec((1, GDN_DV)), _const_spec(w_out.shape),
                  _const_spec((1, d)), _const_spec((1, d))],
        out_specs=tile(d),
        out_shape=jax.ShapeDtypeStruct(x.shape, F32),
        compiler_params=_cparams("parallel", "parallel"),
        name="gdn_out",
    )(o, z, x, norm_g.reshape(1, GDN_DV), w_out.astype(BF16), g.reshape(1, d), b.reshape(1, d))


ATT_HEADS = D_MODEL // 64
ATT_HEAD_DIM = 64
ROPE_DIM = ATT_HEAD_DIM // 4
ROPE_HALF = ROPE_DIM // 2
ATT_NOPE_DIM = ATT_HEAD_DIM - ROPE_DIM
Q_RANK = D_MODEL // 4
KV_RANK = D_MODEL // 8
IDX_HEADS = 8
IDX_DIM = ATT_HEAD_DIM
TOPK_MAX = 256
QK_WIDTH = 2 * LANES
DSA_TS = 512
DSA_QB = 128
DSA_KT = 512
HEAD_GROUP = 4
MASKED = -1e30
LOG2_E = 1.4426950408889634


def _rope_tab_kernel(pos_ref, inv_ref, rot_ref, sgn_ref, cos_ref, sin_ref):
    ang = pos_ref[0].astype(F32) * inv_ref[...]
    cos_ref[0] = jnp.where(rot_ref[...] > 0.0, jnp.cos(ang), 1.0)
    sin_ref[0] = jnp.sin(ang) * sgn_ref[...]


def _rope_tables(positions):
    bn, s = positions.shape
    lane = jnp.arange(LANES)
    inv = ROPE_THETA ** (-jnp.arange(0, ROPE_DIM, 2, dtype=F32) / ROPE_DIM)
    within = lane % ATT_HEAD_DIM
    rot = (within < ROPE_DIM).astype(F32)
    sgn = jnp.where(within < ROPE_HALF, -1.0, 1.0).astype(F32) * rot
    row = lambda v: v.reshape(1, LANES)
    ts = DSA_TS
    tile = pl.BlockSpec((1, ts, LANES), lambda i, j: (i, j, 0))
    return pl.pallas_call(
        _rope_tab_kernel,
        grid=(bn, s // ts),
        in_specs=[pl.BlockSpec((1, ts, 1), lambda i, j: (i, j, 0))] + [_const_spec((1, LANES))] * 3,
        out_specs=[tile, tile],
        out_shape=[jax.ShapeDtypeStruct((bn, s, LANES), F32)] * 2,
        compiler_params=_cparams("parallel", "parallel"),
        name="rope_tables",
    )(positions.reshape(bn, s, 1), row(inv[lane % ROPE_HALF]), row(rot), row(sgn))


def _rope(v, cosm, sinm):
    lane = lax.broadcasted_iota(jnp.int32, (1, LANES), 1)
    low = (lane % ROPE_DIM) < ROPE_HALF
    cols = []
    for c0 in range(0, v.shape[1], LANES):
        blk = v[:, c0:c0 + LANES]
        partner = jnp.where(low, pltpu.roll(blk, LANES - ROPE_HALF, axis=1), pltpu.roll(blk, ROPE_HALF, axis=1))
        cols.append(blk * cosm + partner * sinm)
    return cols[0] if len(cols) == 1 else jnp.concatenate(cols, axis=1)


def _rms_norm(v, g):
    return v * lax.rsqrt(jnp.mean(v * v, -1, keepdims=True) + RMS_EPS) * g


def _dsa_proj_kernel(x_ref, cos_ref, sin_ref, wdq_ref, qn_ref, wuq_ref, wcomb_ref, wdkv_ref, kvn_ref, wkr_ref,
                     wiq_ref, wik_ref, ikg_ref, ikb_ref, prot_ref, wiwt_ref,
                     q2_ref, k2_ref, qi_ref, ki_ref, wit_ref):
    xb = x_ref[0].astype(BF16)
    cosm, sinm = cos_ref[0], sin_ref[0]
    cq = _rms_norm(_dot(xb, wdq_ref[...]), qn_ref[...]).astype(BF16)
    q = _rope(_dot(cq, wuq_ref[...]), cosm, sinm) * (ATT_HEAD_DIM ** -0.5 * LOG2_E)
    qb16 = q.astype(BF16)
    for g in range(ATT_HEADS // HEAD_GROUP):
        q2_ref[0, :, g * HEAD_GROUP * QK_WIDTH:(g + 1) * HEAD_GROUP * QK_WIDTH] = _dot(
            qb16[:, g * HEAD_GROUP * ATT_HEAD_DIM:(g + 1) * HEAD_GROUP * ATT_HEAD_DIM], wcomb_ref[g]).astype(BF16)
    ckv = _rms_norm(_dot(xb, wdkv_ref[...]), kvn_ref[...])
    kr = _rope(_dot(xb, wkr_ref[...]), cosm, sinm)
    k2_ref[0] = jnp.concatenate([ckv, kr], axis=1).astype(BF16)
    qi = _rope(_dot(cq, wiq_ref[...]), cosm, sinm).astype(BF16)
    for h in range(IDX_HEADS):
        qi_ref[0, h] = qi[:, h * IDX_DIM:(h + 1) * IDX_DIM]
    kin = _layer_norm(_dot(xb, wik_ref[...]), ikg_ref[...], ikb_ref[...])
    ki = kin * cosm[:, :IDX_DIM] + _hdot(kin, prot_ref[...]) * sinm[:, :IDX_DIM]
    ki_ref[0] = ki.astype(BF16)
    wit = lax.dot_general(wiwt_ref[...], xb, NT_DIMS, preferred_element_type=F32)
    wit_ref[0] = wit * (IDX_HEADS * IDX_DIM) ** -0.5


def _dsa_attn_kernel(q2_ref, k2_ref, qi_ref, ki_ref, wit_ref, o_ref,
                     key_ref, mask_ref, q2s_ref, lga_ref, lgb_ref, pa_ref, pb_ref, acc_ref):
    qb, kt_sz = DSA_QB, DSA_KT
    i = pl.program_id(1)
    n_tiles = (i * qb) // kt_sz + 1
    q_pos = i * qb + lax.broadcasted_iota(jnp.int32, (1, qb), 1)

    def score_tile(t, carry):
        r0 = pl.multiple_of(t * kt_sz, kt_sz)
        kblk = ki_ref[0, pl.ds(r0, kt_sz), :]
        acc = jnp.zeros((kt_sz, qb), F32)
        for h in range(IDX_HEADS):
            sc = lax.dot_general(kblk, qi_ref[0, h], NT_DIMS, preferred_element_type=F32)
            acc = acc + jnp.maximum(sc, 0.0) * wit_ref[0, h:h + 1, :]
        k_pos = r0 + lax.broadcasted_iota(jnp.int32, (kt_sz, 1), 0)
        acc = jnp.where(k_pos <= q_pos, acc, -jnp.inf)
        bits = pltpu.bitcast(acc, jnp.int32)
        key_ref[pl.ds(r0, kt_sz), :] = bits ^ ((bits >> 31) & jnp.int32(0x7FFFFFFF))
        return carry

    lax.fori_loop(0, n_tiles, score_tile, 0)

    def count(pred):
        def body(t, acc):
            r0 = pl.multiple_of(t * kt_sz, kt_sz)
            k_pos = r0 + lax.broadcasted_iota(jnp.int32, (kt_sz, 1), 0)
            hit = jnp.where(pred(key_ref[pl.ds(r0, kt_sz), :], k_pos), 1, 0).astype(jnp.int32)
            return acc + jnp.sum(hit.reshape(kt_sz // SUBLANES, SUBLANES, qb), axis=0)
        acc = lax.fori_loop(0, n_tiles, body, jnp.zeros((SUBLANES, qb), jnp.int32))
        return jnp.sum(acc, axis=0, keepdims=True)

    int_min = jnp.int32(-2 ** 31)
    topk = jnp.int32(TOPK_MAX)

    def search_bit(b, thr):
        cand = thr + lax.shift_left(jnp.int32(1), jnp.int32(31) - b)
        return jnp.where(count(lambda kb, kp: kb >= cand) >= topk, cand, thr)

    few_keys = (i + 1) * qb <= TOPK_MAX
    thr0 = jnp.full((1, qb), int_min, jnp.int32)
    thr = lax.cond(few_keys, lambda: thr0, lambda: lax.fori_loop(0, 32, search_bit, thr0))

    need = topk - count(lambda kb, kp: kb > thr)
    n_eq = count(lambda kb, kp: kb == thr)
    full_cut = jnp.full((1, qb), jnp.int32(2 ** 30), jnp.int32)

    def search_cut():
        def cut_bit(b, cut):
            cand = cut + lax.shift_left(jnp.int32(1), jnp.int32(12) - b)
            below = count(lambda kb, kp: (kb == thr) & (kp < cand))
            return jnp.where(below < need, cand, cut)
        return lax.fori_loop(0, 13, cut_bit, jnp.zeros((1, qb), jnp.int32))

    tied = jnp.logical_and(jnp.logical_not(few_keys), jnp.max(jnp.where(n_eq != need, 1, 0)) > 0)
    cut = lax.cond(tied, search_cut, lambda: full_cut)

    def mask_tile(t, carry):
        r0 = pl.multiple_of(t * kt_sz, kt_sz)
        kb = key_ref[pl.ds(r0, kt_sz), :]
        k_pos = r0 + lax.broadcasted_iota(jnp.int32, (kt_sz, 1), 0)
        keep = ((kb > thr) | ((kb == thr) & (k_pos <= cut))) & (k_pos <= q_pos)
        mask_ref[:, pl.ds(r0, kt_sz)] = jnp.where(keep, 0.0, MASKED).astype(F32).T
        return carry

    lax.fori_loop(0, n_tiles, mask_tile, 0)

    nh = ATT_HEADS
    half = kt_sz // 2
    for h in range(nh):
        q2s_ref[h * qb:(h + 1) * qb, :] = q2_ref[0, :, h * QK_WIDTH:(h + 1) * QK_WIDTH]
    acc_ref[...] = jnp.zeros(acc_ref.shape, F32)
    vlane = lax.broadcasted_iota(jnp.int32, (1, QK_WIDTH), 1)
    ones_col = jnp.where(vlane == KV_RANK, 1.0, 0.0).astype(BF16)

    def logits_into(dst_ref, key0):
        dst_ref[...] = lax.dot_general(q2s_ref[...], k2_ref[0, pl.ds(pl.multiple_of(key0, half), half), :],
                                       NT_DIMS, preferred_element_type=F32)

    def softmax_pv(src_ref, prob_ref, key0, m):
        key0 = pl.multiple_of(key0, half)
        mask = mask_ref[:, pl.ds(key0, half)]
        wide = lambda v: jnp.concatenate([v] * (half // LANES), axis=1)
        m_parts, alpha_parts = [], []
        for h in range(nh):
            rows = slice(h * qb, (h + 1) * qb)
            lg = src_ref[rows, :] + mask
            m_new = jnp.maximum(m[rows], jnp.max(lg, -1, keepdims=True))
            prob_ref[rows, :] = jnp.exp2(lg - wide(m_new)).astype(BF16)
            alpha_parts.append(jnp.exp2(m[rows] - m_new))
            m_parts.append(m_new)
        v1 = jnp.where(vlane < KV_RANK, k2_ref[0, pl.ds(key0, half), :], ones_col)
        alpha = jnp.concatenate(alpha_parts, axis=0)
        acc_ref[...] = acc_ref[...] * jnp.concatenate([alpha] * (QK_WIDTH // LANES), axis=1) + _dot(prob_ref[...], v1)
        return jnp.concatenate(m_parts, axis=0)

    logits_into(lga_ref, 0)
    last_key0 = (n_tiles - 1) * kt_sz

    def attn_tile(t, m):
        key0 = t * kt_sz
        logits_into(lgb_ref, key0 + half)
        m = softmax_pv(lga_ref, pa_ref, key0, m)
        logits_into(lga_ref, jnp.minimum(key0 + kt_sz, last_key0))
        return softmax_pv(lgb_ref, pb_ref, key0 + half, m)

    lax.fori_loop(0, n_tiles, attn_tile, jnp.full((nh * qb, LANES), MASKED, F32))
    for h in range(nh):
        acc = acc_ref[h * qb:(h + 1) * qb, :]
        o_ref[0, :, h * KV_RANK:(h + 1) * KV_RANK] = (
            acc[:, :KV_RANK] / acc[:, KV_RANK:KV_RANK + 1]).astype(o_ref.dtype)


def _dsa_out_kernel(o_ref, x_ref, wuv_ref, wo_ref, g_ref, b_ref, y_ref):
    o = jnp.concatenate(
        [_dot(o_ref[0, :, g * HEAD_GROUP * KV_RANK:(g + 1) * HEAD_GROUP * KV_RANK], wuv_ref[g])
         for g in range(ATT_HEADS // HEAD_GROUP)], axis=1).astype(BF16)
    mix = _dot(o, wo_ref[...])
    y_ref[0] = _layer_norm(DN_ALPHA * x_ref[0] + mix, g_ref[...], b_ref[...])


def _block_diag(blocks):
    n, r, c = blocks.shape
    idx = jnp.arange(n)
    return jnp.zeros((n, r, n, c), blocks.dtype).at[idx, :, idx, :].set(blocks).reshape(n * r, n * c)


def _dsa_ln(x, positions, w_dq, q_norm, w_uq, w_dkv, kv_norm, w_kr, w_uk, w_uv, w_o,
            w_iq, w_ik, ik_g, ik_b, w_iw, g, b):
    bn, s, d = x.shape
    ts, nh = DSA_TS, ATT_HEADS
    cosm, sinm = _rope_tables(positions)

    head_map = jnp.zeros((nh, ATT_HEAD_DIM, QK_WIDTH), F32)
    head_map = head_map.at[:, ROPE_DIM:, :KV_RANK].set(w_uk)
    head_map = head_map.at[:, :ROPE_DIM, KV_RANK:KV_RANK + ROPE_DIM].set(jnp.eye(ROPE_DIM, dtype=F32))
    groups = range(0, nh, HEAD_GROUP)
    w_comb = jnp.stack([_block_diag(head_map[h:h + HEAD_GROUP]) for h in groups]).astype(BF16)
    w_kr_pad = jnp.zeros((d, LANES), F32).at[:, :ROPE_DIM].set(w_kr).astype(BF16)
    lane = jnp.arange(IDX_DIM)
    partner = jnp.where(lane < ROPE_HALF, lane + ROPE_HALF, lane - ROPE_HALF)
    p_rot = ((lane[:, None] == partner[None, :]) & (lane[None, :] < ROPE_DIM)).astype(F32)

    tile = lambda w: pl.BlockSpec((1, ts, w), lambda i, j: (i, j, 0))
    consts = [w_dq.astype(BF16), q_norm.reshape(1, -1), w_uq.astype(BF16), w_comb, w_dkv.astype(BF16),
              kv_norm.reshape(1, -1), w_kr_pad, w_iq.astype(BF16), w_ik.astype(BF16), ik_g.reshape(1, -1),
              ik_b.reshape(1, -1), p_rot, w_iw.T.astype(BF16)]
    q2, k2, qi, ki, wit = pl.pallas_call(
        _dsa_proj_kernel,
        grid=(bn, s // ts),
        in_specs=[tile(d), tile(LANES), tile(LANES)] + [_const_spec(c.shape) for c in consts],
        out_specs=[tile(nh * QK_WIDTH), tile(QK_WIDTH),
                   pl.BlockSpec((1, IDX_HEADS, ts, IDX_DIM), lambda i, j: (i, 0, j, 0)),
                   tile(IDX_DIM), pl.BlockSpec((1, IDX_HEADS, ts), lambda i, j: (i, 0, j))],
        out_shape=[jax.ShapeDtypeStruct((bn, s, nh * QK_WIDTH), BF16), jax.ShapeDtypeStruct((bn, s, QK_WIDTH), BF16),
                   jax.ShapeDtypeStruct((bn, IDX_HEADS, s, IDX_DIM), BF16),
                   jax.ShapeDtypeStruct((bn, s, IDX_DIM), BF16), jax.ShapeDtypeStruct((bn, IDX_HEADS, s), F32)],
        compiler_params=_cparams("parallel", "parallel"),
        name="dsa_proj",
    )(x, cosm, sinm, *consts)

    qb = DSA_QB
    o_lat = pl.pallas_call(
        _dsa_attn_kernel,
        grid=(bn, s // qb),
        in_specs=[pl.BlockSpec((1, qb, nh * QK_WIDTH), lambda i, j: (i, j, 0)),
                  pl.BlockSpec((1, s, QK_WIDTH), lambda i, j: (i, 0, 0)),
                  pl.BlockSpec((1, IDX_HEADS, qb, IDX_DIM), lambda i, j: (i, 0, j, 0)),
                  pl.BlockSpec((1, s, IDX_DIM), lambda i, j: (i, 0, 0)),
                  pl.BlockSpec((1, IDX_HEADS, qb), lambda i, j: (i, 0, j))],
        out_specs=pl.BlockSpec((1, qb, nh * KV_RANK), lambda i, j: (i, j, 0)),
        out_shape=jax.ShapeDtypeStruct((bn, s, nh * KV_RANK), BF16),
        scratch_shapes=[pltpu.VMEM((s, qb), jnp.int32), pltpu.VMEM((qb, s), F32),
                        pltpu.VMEM((nh * qb, QK_WIDTH), BF16),
                        pltpu.VMEM((nh * qb, DSA_KT // 2), F32), pltpu.VMEM((nh * qb, DSA_KT // 2), F32),
                        pltpu.VMEM((nh * qb, DSA_KT // 2), BF16), pltpu.VMEM((nh * qb, DSA_KT // 2), BF16),
                        pltpu.VMEM((nh * qb, QK_WIDTH), F32)],
        compiler_params=_cparams("parallel", "parallel"),
        name="dsa_attn",
    )(q2, k2, qi, ki, wit)

    w_uv_bd = jnp.stack([_block_diag(w_uv[h:h + HEAD_GROUP]) for h in groups]).astype(BF16)
    return pl.pallas_call(
        _dsa_out_kernel,
        grid=(bn, s // ts),
        in_specs=[tile(nh * KV_RANK), tile(d), _const_spec(w_uv_bd.shape), _const_spec(w_o.shape),
                  _const_spec((1, d)), _const_spec((1, d))],
        out_specs=tile(d),
        out_shape=jax.ShapeDtypeStruct(x.shape, F32),
        compiler_params=_cparams("parallel", "parallel"),
        name="dsa_out",
    )(o_lat, x, w_uv_bd, w_o.astype(BF16), g.reshape(1, d), b.reshape(1, d))


def kernel(x, positions, ln_g, ln_b, ffn_w_gate, ffn_w_up, ffn_w_down,
           conv_w_in, conv_w_dw, conv_ln_g, conv_ln_b, conv_w_out,
           sc_w_in, sc_w_conv, sc_w_out,
           dsa_w_dq, dsa_q_norm, dsa_w_uq, dsa_w_dkv, dsa_kv_norm, dsa_w_kr,
           dsa_w_uk, dsa_w_uv, dsa_w_o, dsa_w_iq, dsa_w_ik, dsa_ik_ln_g, dsa_ik_ln_b, dsa_w_iw,
           gdn_w_in, gdn_w_conv, gdn_a_log, gdn_dt_bias, gdn_norm_g, gdn_w_out):
    bn, s, d = x.shape

    def ffn(h, i, half):
        return _ffn_ln(h.reshape(bn * s, d), ffn_w_gate[i, half], ffn_w_up[i, half], ffn_w_down[i, half],
                       ln_g[i, 2 * half], ln_b[i, 2 * half]).reshape(bn, s, d)

    for i in range(DEPTH):
        m, j = i % N_MIXERS, i // N_MIXERS
        x = ffn(x, i, 0)
        g, b = ln_g[i, 1], ln_b[i, 1]
        if m == 0:
            x = _conformer_ln(x, conv_w_in[j], conv_w_dw[j], conv_ln_g[j], conv_ln_b[j], conv_w_out[j], g, b)
        elif m == 1:
            x = _sconv_ln(x, sc_w_in[j], sc_w_conv[j], sc_w_out[j], g, b)
        elif m == 2:
            x = _dsa_ln(x, positions, dsa_w_dq[j], dsa_q_norm[j], dsa_w_uq[j], dsa_w_dkv[j], dsa_kv_norm[j],
                        dsa_w_kr[j], dsa_w_uk[j], dsa_w_uv[j], dsa_w_o[j], dsa_w_iq[j], dsa_w_ik[j],
                        dsa_ik_ln_g[j], dsa_ik_ln_b[j], dsa_w_iw[j], g, b)
        else:
            x = _gdn_ln(x, gdn_w_in[j], gdn_w_conv[j], gdn_a_log[j], gdn_dt_bias[j], gdn_norm_g[j],
                        gdn_w_out[j], g, b)
        x = ffn(x, i, 1)
    return x
```

```python
import functools

import jax
import jax.numpy as jnp
from jax import lax
from jax.experimental import pallas as pl
from jax.experimental.pallas import tpu as pltpu

D_MODEL = 1024
DEPTH = 4
N_MIXERS = 4
FFN_DIM = ((8 * D_MODEL // 3 + 127) // 128) * 128
DN_ALPHA = (2.0 * DEPTH) ** 0.25
ROPE_THETA = 500000.0
LN_EPS = 1e-5
RMS_EPS = 1e-6
CONF_KERNEL = 31
SCONV_KERNEL = 3

BF16 = jnp.bfloat16
F32 = jnp.float32

V7X_VMEM_LIMIT_BYTES = 56 * 1024 * 1024
SUBLANES = 8
LANES = 128


def _cparams(*sem):
    return pltpu.CompilerParams(dimension_semantics=sem, vmem_limit_bytes=V7X_VMEM_LIMIT_BYTES)


def _layer_norm(y, g, b):
    mu = jnp.mean(y, -1, keepdims=True)
    yc = y - mu
    var = jnp.mean(yc * yc, -1, keepdims=True)
    return yc * lax.rsqrt(var + LN_EPS) * g + b


def _silu(h):
    return h * jax.nn.sigmoid(h)


def _dot(a, b):
    return jnp.dot(a, b, preferred_element_type=F32)


def _const_spec(shape):
    nd = len(shape)
    return pl.BlockSpec(shape, lambda *_: (0,) * nd, pipeline_mode=pl.Buffered(1))


FFN_TM = 512
FFN_CHUNK = 768


def _ffn_kernel(x_ref, wg_ref, wu_ref, wd_ref, g_ref, b_ref, o_ref):
    x = x_ref[...]
    xb = x.astype(BF16)
    bounds = list(range(0, FFN_DIM, FFN_CHUNK)) + [FFN_DIM]
    gate_up = lambda c: (_dot(xb, wg_ref[:, bounds[c]:bounds[c + 1]]), _dot(xb, wu_ref[:, bounds[c]:bounds[c + 1]]))
    acc = jnp.zeros(x.shape, F32)
    h, u = gate_up(0)
    for c in range(len(bounds) - 1):
        nxt = gate_up(c + 1) if c + 2 < len(bounds) else None
        a = (_silu(h) * u).astype(BF16)
        acc = acc + _dot(a, wd_ref[bounds[c]:bounds[c + 1], :])
        if nxt is not None:
            h, u = nxt
    o_ref[...] = _layer_norm(DN_ALPHA * x + 0.5 * acc, g_ref[...], b_ref[...])


def _ffn_ln(x2, wg, wu, wd, g, b):
    t, d = x2.shape
    row = pl.BlockSpec((FFN_TM, d), lambda i: (i, 0))
    return pl.pallas_call(
        _ffn_kernel,
        grid=(t // FFN_TM,),
        in_specs=[row, _const_spec(wg.shape), _const_spec(wu.shape), _const_spec(wd.shape),
                  _const_spec((1, d)), _const_spec((1, d))],
        out_specs=row,
        out_shape=jax.ShapeDtypeStruct((t, d), F32),
        compiler_params=_cparams("parallel"),
        name="ffn_ln",
    )(x2, wg.astype(BF16), wu.astype(BF16), wd.astype(BF16), g.reshape(1, d), b.reshape(1, d))


def _carry_halo(buf_ref, halo, ts, first_tile):
    @pl.when(first_tile)
    def _():
        buf_ref[0:halo, :] = jnp.zeros((halo, buf_ref.shape[-1]), F32)

    @pl.when(jnp.logical_not(first_tile))
    def _():
        buf_ref[0:halo, :] = buf_ref[ts:ts + halo, :]


def _dwconv_taps(buf_ref, u, w_ref, width, halo, ts):
    buf_ref[halo:halo + ts, :] = u
    off = halo - (width - 1)
    acc = w_ref[width - 1:width, :] * u
    for k in range(width - 1):
        acc = acc + w_ref[k:k + 1, :] * buf_ref[off + k:off + k + ts, :]
    return acc


def _causal_dwconv(buf_ref, u, w_ref, width, halo, ts, first_tile):
    _carry_halo(buf_ref, halo, ts, first_tile)
    return _dwconv_taps(buf_ref, u, w_ref, width, halo, ts)


def _causal_dwconv_wide(buf_ref, sh_ref, out_ref, u, w_ref, width, halo, ts, first_tile):
    c = u.shape[-1]

    @pl.when(first_tile)
    def _():
        buf_ref[0:halo, :] = jnp.zeros((halo, c), F32)

    @pl.when(jnp.logical_not(first_tile))
    def _():
        buf_ref[0:halo, :] = buf_ref[ts:ts + halo, :]

    buf_ref[halo:halo + ts, :] = u
    off = halo - (width - 1)
    n = halo + ts - SUBLANES
    for b in range(1, SUBLANES):
        sh_ref[b - 1, 0:n, :] = buf_ref[b:b + n, :]

    def block(i, carry):
        r = pl.multiple_of(i * DWCONV_ROWS, DWCONV_ROWS)
        for c0 in range(0, c, DWCONV_COLS):
            cols = slice(c0, c0 + DWCONV_COLS)
            acc = w_ref[width - 1:width, cols] * buf_ref[pl.ds(halo + r, DWCONV_ROWS), cols]
            for k in range(width - 1):
                a, b = divmod(off + k, SUBLANES)
                src = buf_ref if b == 0 else sh_ref.at[b - 1]
                acc = acc + w_ref[k:k + 1, cols] * src[pl.ds(a * SUBLANES + r, DWCONV_ROWS), cols]
            out_ref[pl.ds(r, DWCONV_ROWS), cols] = acc
        return carry

    lax.fori_loop(0, ts // DWCONV_ROWS, block, 0)
    return out_ref[...]


CONV_TS = 512
CONF_HALO = 32
DWCONV_ROWS = 64
DWCONV_COLS = 256


def _conf_kernel(x_ref, win_ref, wdw_ref, lng_ref, lnb_ref, wout_ref, g_ref, b_ref, o_ref, buf_ref, sh_ref, cv_ref):
    d = D_MODEL
    x = x_ref[0]
    uu = _dot(x.astype(BF16), win_ref[...])
    u = uu[:, :d] * jax.nn.sigmoid(uu[:, d:])
    cv = _causal_dwconv_wide(buf_ref, sh_ref, cv_ref, u, wdw_ref, CONF_KERNEL, CONF_HALO, CONV_TS,
                             pl.program_id(1) == 0)
    v = _silu(_layer_norm(cv, lng_ref[...], lnb_ref[...]))
    mix = _dot(v.astype(BF16), wout_ref[...])
    o_ref[0] = _layer_norm(DN_ALPHA * x + mix, g_ref[...], b_ref[...])


def _conformer_ln(x, w_in, w_dw, ln_g, ln_b, w_out, g, b):
    bn, s, d = x.shape
    tile = pl.BlockSpec((1, CONV_TS, d), lambda i, j: (i, j, 0))
    return pl.pallas_call(
        _conf_kernel,
        grid=(bn, s // CONV_TS),
        in_specs=[tile, _const_spec(w_in.shape), _const_spec(w_dw.shape), _const_spec((1, d)),
                  _const_spec((1, d)), _const_spec(w_out.shape), _const_spec((1, d)), _const_spec((1, d))],
        out_specs=tile,
        out_shape=jax.ShapeDtypeStruct(x.shape, F32),
        scratch_shapes=[pltpu.VMEM((CONF_HALO + CONV_TS, d), F32),
                        pltpu.VMEM((SUBLANES - 1, CONF_HALO + CONV_TS, d), F32), pltpu.VMEM((CONV_TS, d), F32)],
        compiler_params=_cparams("parallel", "arbitrary"),
        name="conformer_ln",
    )(x, w_in.astype(BF16), w_dw, ln_g.reshape(1, d), ln_b.reshape(1, d), w_out.astype(BF16),
      g.reshape(1, d), b.reshape(1, d))


SCONV_HALO = SUBLANES


def _sconv_kernel(x_ref, win_ref, wc_ref, wout_ref, g_ref, b_ref, o_ref, buf_ref):
    d = D_MODEL
    x = x_ref[0]
    bch = _dot(x.astype(BF16), win_ref[...])
    cv = _causal_dwconv(buf_ref, bch[:, d:2 * d] * bch[:, 2 * d:], wc_ref, SCONV_KERNEL, SCONV_HALO,
                        CONV_TS, pl.program_id(1) == 0)
    y = bch[:, :d] * cv
    mix = _dot(y.astype(BF16), wout_ref[...])
    o_ref[0] = _layer_norm(DN_ALPHA * x + mix, g_ref[...], b_ref[...])


def _sconv_ln(x, w_in, w_conv, w_out, g, b):
    bn, s, d = x.shape
    tile = pl.BlockSpec((1, CONV_TS, d), lambda i, j: (i, j, 0))
    return pl.pallas_call(
        _sconv_kernel,
        grid=(bn, s // CONV_TS),
        in_specs=[tile, _const_spec(w_in.shape), _const_spec(w_conv.shape), _const_spec(w_out.shape),
                  _const_spec((1, d)), _const_spec((1, d))],
        out_specs=tile,
        out_shape=jax.ShapeDtypeStruct(x.shape, F32),
        scratch_shapes=[pltpu.VMEM((SCONV_HALO + CONV_TS, d), F32)],
        compiler_params=_cparams("parallel", "arbitrary"),
        name="sconv_ln",
    )(x, w_in.astype(BF16), w_conv, w_out.astype(BF16), g.reshape(1, d), b.reshape(1, d))


GDN_K_HEADS = D_MODEL // 128
GDN_V_HEADS = 2 * GDN_K_HEADS
GDN_DK = 128
GDN_DV = 128
GDN_CONV = 4
GDN_CHUNK = 64
GDN_NK = GDN_K_HEADS * GDN_DK
GDN_NV = GDN_V_HEADS * GDN_DV
GDN_QKV = 2 * GDN_NK + GDN_NV
GDN_TS = 512
GDN_TC = 256
GDN_TP = 512
GDN_HALO = SUBLANES
HIGHEST = lax.Precision.HIGHEST
NT_DIMS = (((1,), (1,)), ((), ()))
TN_DIMS = (((0,), (0,)), ((), ()))


def _hdot(a, b):
    return jnp.dot(a, b, preferred_element_type=F32, precision=HIGHEST)


def _softplus(v):
    return jnp.maximum(v, 0.0) + jnp.log1p(jnp.exp(-jnp.abs(v)))


def _gdn_in_kernel(x_ref, wqkv_ref, wconv_ref, wz_ref, wba_ref, wbat_ref, alog_r_ref, dt_r_ref, alog_c_ref,
                   dt_c_ref, q_ref, k_ref, v_ref, z_ref, bcol_ref, gcol_ref, rs_ref, egl_ref, buf_ref):
    ts = GDN_TS
    xb = x_ref[0].astype(BF16)
    first = pl.program_id(1) == 0

    _carry_halo(buf_ref, GDN_HALO, ts, first)
    n_parts = GDN_QKV // GDN_NK
    proj = lambda p: _dot(xb, wqkv_ref[:, p * GDN_NK:(p + 1) * GDN_NK])

    def conv_silu(p, u):
        cols = slice(p * GDN_NK, (p + 1) * GDN_NK)
        return _silu(_dwconv_taps(buf_ref.at[:, cols], u, wconv_ref.at[:, cols], GDN_CONV, GDN_HALO, ts))

    u = proj(0)
    for p in range(n_parts):
        u_next = proj(p + 1) if p + 1 < n_parts else _dot(xb, wz_ref[...])
        y = conv_silu(p, u)
        if p < 2:
            o_ref, scale = ((q_ref, GDN_DK ** -0.5), (k_ref, 1.0))[p]
            for h in range(GDN_K_HEADS):
                hd = y[:, h * GDN_DK:(h + 1) * GDN_DK]
                o_ref[0, :, h * GDN_DK:(h + 1) * GDN_DK] = (
                    hd * (lax.rsqrt(jnp.sum(hd * hd, -1, keepdims=True) + RMS_EPS) * scale))
        else:
            v_ref[0, :, (p - 2) * GDN_NK:(p - 1) * GDN_NK] = y
        u = u_next
    z_ref[0] = u.astype(z_ref.dtype)

    nh = GDN_V_HEADS
    ba = _dot(xb, wba_ref[...])
    beta_c = jax.nn.sigmoid(ba[:, :nh])
    g_c = -jnp.exp(alog_r_ref[...]) * _softplus(ba[:, nh:] + dt_r_ref[...])
    bat = lax.dot_general(wbat_ref[...], xb, NT_DIMS, preferred_element_type=F32)
    beta_r = jax.nn.sigmoid(bat[:nh])
    g_r = -jnp.exp(alog_c_ref[...]) * _softplus(bat[nh:] + dt_c_ref[...])
    ri = lax.broadcasted_iota(jnp.int32, (ts, ts), 0)
    ci = lax.broadcasted_iota(jnp.int32, (ts, ts), 1)
    same = (ri // GDN_CHUNK) == (ci // GDN_CHUNK)
    low = jnp.where(same & (ci <= ri), 1.0, 0.0).astype(F32)
    upp = jnp.where(same & (ri <= ci), 1.0, 0.0).astype(F32)
    gc_c = _hdot(low, g_c)
    gc_r = _hdot(g_r, upp)
    bcol_ref[0] = beta_c
    gcol_ref[0] = gc_c
    rs_ref[0, 0:nh, :] = beta_r
    rs_ref[0, nh:2 * nh, :] = gc_r
    nchunk = ts // GDN_CHUNK
    cr = lax.broadcasted_iota(jnp.int32, (nchunk, ts), 0)
    ct = lax.broadcasted_iota(jnp.int32, (nchunk, ts), 1)
    chunk_sum = jnp.where(ct // GDN_CHUNK == cr, 1.0, 0.0).astype(F32)
    gl = _hdot(chunk_sum, g_c)
    er = lax.broadcasted_iota(jnp.int32, (nh, GDN_NV), 0)
    ec = lax.broadcasted_iota(jnp.int32, (nh, GDN_NV), 1)
    expand = jnp.where(ec // GDN_DV == er, 1.0, 0.0).astype(F32)
    egl_ref[0] = jnp.exp(_hdot(gl, expand))


def _unit_lower_inverses(mats, eye, order):
    n = mats[0].shape[0]
    ts = [eye - a for a in mats]
    ps = [_dot(ab, ab) for ab in (a.astype(BF16) for a in mats)]
    for _ in range(order.bit_length() - 3):
        pbs = [p.astype(BF16) for p in ps]
        tps = [_dot(jnp.concatenate([t.astype(BF16), pb], axis=0), pb) for t, pb in zip(ts, pbs)]
        ts = [t + tp[:n] for t, tp in zip(ts, tps)]
        ps = [tp[n:] for tp in tps]
    return [t + _dot(t.astype(BF16), p.astype(BF16)) for t, p in zip(ts, ps)]


def _gdn_pre_kernel(q_ref, k_ref, v_ref, bcol_ref, gcol_ref, rs_ref, w_ref, u_ref, qh_ref, kt_ref, in_ref):
    tc, c, nh = GDN_TC, GDN_CHUNK, GDN_V_HEADS
    j = pl.program_id(1)
    head_lane = lax.broadcasted_iota(jnp.int32, (tc, nh), 1)
    ri = lax.broadcasted_iota(jnp.int32, (tc, tc), 0)
    ci = lax.broadcasted_iota(jnp.int32, (tc, tc), 1)
    same = (ri // c) == (ci // c)
    tril = same & (ri >= ci)
    strict = same & (ri > ci)
    eye = jnp.where(ri == ci, 1.0, 0.0).astype(F32)
    lane = lax.broadcasted_iota(jnp.int32, (1, LANES), 1)

    chains = []
    for g in range(GDN_TP // tc):
        rows = slice(g * tc, (g + 1) * tc)
        q, k = q_ref[0, rows, :], k_ref[0, rows, :]
        qb, kb = q.astype(BF16), k.astype(BF16)
        kk = lax.dot_general(kb, kb, NT_DIMS, preferred_element_type=F32)
        qk = lax.dot_general(qb, kb, NT_DIMS, preferred_element_type=F32)
        for hh in range(2):
            h = 2 * j + hh
            sel = head_lane == h
            beta_c = jnp.sum(jnp.where(sel, bcol_ref[0, rows, :], 0.0), -1, keepdims=True)
            gc_c = jnp.sum(jnp.where(sel, gcol_ref[0, rows, :], 0.0), -1, keepdims=True)
            gl_c = jnp.concatenate([jnp.broadcast_to(gc_c[(m + 1) * c - 1:(m + 1) * c, :], (c, 1))
                                    for m in range(tc // c)], axis=0)
            beta_r = rs_ref[0, pl.ds(h, 1), rows]
            gc_r = rs_ref[0, pl.ds(nh + h, 1), rows]
            decay = jnp.exp(jnp.where(tril, gc_c - gc_r, -jnp.inf))
            chains.append(dict(a=jnp.where(strict, beta_c * kk * decay, 0.0), rows=rows, hh=hh, q=q, k=k, kb=kb,
                               qk=qk, decay=decay, beta_r=beta_r, gc_r=gc_r, gc_c=gc_c, gl_c=gl_c))
    inverses = _unit_lower_inverses([ch["a"] for ch in chains], eye, c)

    intra_prev = None
    for ch, tm in zip(chains, inverses):
        rows, hh = ch["rows"], ch["hh"]
        cols = slice(hh * GDN_DV, (hh + 1) * GDN_DV)
        vh = v_ref[0, rows, cols].astype(BF16)
        u_ref[0, rows, cols] = _dot((tm * ch["beta_r"]).astype(BF16), vh)
        w_ref[0, rows, cols] = _dot((tm * (ch["beta_r"] * jnp.exp(ch["gc_r"]))).astype(BF16), ch["kb"]).astype(BF16)
        qh_ref[0, rows, cols] = (ch["q"] * jnp.exp(ch["gc_c"])).astype(BF16)
        kt_ref[0, rows, cols] = (ch["k"] * jnp.exp(ch["gl_c"] - ch["gc_c"])).astype(BF16)
        intra = jnp.where(tril, ch["qk"] * ch["decay"], 0.0)
        half = intra[:, :LANES] + intra[:, LANES:]
        full = half + pltpu.roll(half, c, axis=1)
        if hh == 1:
            in_ref[0, rows, :] = jnp.where(lane < c, intra_prev, full).astype(BF16)
        intra_prev = full


def _gdn_scan_kernel(w_ref, u_ref, qh_ref, kt_ref, in_ref, egl_ref, o_ref, s_ref):
    c = GDN_CHUNK
    pw = 2 * GDN_DV
    ri = lax.broadcasted_iota(jnp.int32, (pw, pw), 0)
    ci = lax.broadcasted_iota(jnp.int32, (pw, pw), 1)
    block_diag = (ri // GDN_DK) == (ci // GDN_DV)

    @pl.when(pl.program_id(1) == 0)
    def _():
        s_ref[...] = jnp.zeros(s_ref.shape, F32)

    def step(n, carry):
        r0 = pl.multiple_of(n * c, c)
        rows = pl.ds(r0, c)
        zero = jnp.zeros((c, GDN_DV), BF16)
        pairs = range(GDN_K_HEADS)
        cols = [slice(p * pw, (p + 1) * pw) for p in pairs]
        ss = [s_ref[p] for p in pairs]
        rs = [_dot(jnp.concatenate([w_ref[0, rows, cols[p]], qh_ref[0, rows, cols[p]]], axis=0),
                   ss[p].astype(BF16)) for p in pairs]
        vns = [(u_ref[0, rows, cols[p]] - rs[p][:c]).astype(BF16) for p in pairs]
        vbds = [jnp.concatenate([jnp.concatenate([vn[:, :GDN_DV], zero], axis=1),
                                 jnp.concatenate([zero, vn[:, GDN_DV:]], axis=1)], axis=0) for vn in vns]
        outs = [rs[p][c:] + _dot(in_ref[0, rows, p * 2 * c:(p + 1) * 2 * c], vbds[p]) for p in pairs]
        upds = [lax.dot_general(kt_ref[0, rows, cols[p]], vns[p], TN_DIMS, preferred_element_type=F32)
                for p in pairs]
        o_ref[0, rows, :] = jnp.concatenate(outs, axis=1)
        s_ref[...] = jnp.stack([ss[p] * egl_ref[0, pl.ds(n, 1), cols[p]] + jnp.where(block_diag, upds[p], 0.0)
                                for p in pairs], axis=0)
        return carry

    lax.fori_loop(0, GDN_TS // c, step, 0)


def _gdn_out_kernel(o_ref, z_ref, x_ref, ng_ref, wout_ref, g_ref, b_ref, y_ref):
    parts = []
    for h in range(GDN_V_HEADS):
        cols = slice(h * GDN_DV, (h + 1) * GDN_DV)
        oh = o_ref[0, :, cols]
        oh = oh * lax.rsqrt(jnp.mean(oh * oh, -1, keepdims=True) + RMS_EPS) * ng_ref[...]
        parts.append((oh * _silu(z_ref[0, :, cols].astype(F32))).astype(BF16))
    mix = _dot(jnp.concatenate(parts, axis=1), wout_ref[...])
    y_ref[0] = _layer_norm(DN_ALPHA * x_ref[0] + mix, g_ref[...], b_ref[...])


def _gdn_ln(x, w_in, w_conv, a_log, dt_bias, norm_g, w_out, g, b):
    bn, s, d = x.shape
    nh, ts = GDN_V_HEADS, GDN_TS
    nchunks = s // GDN_CHUNK
    w_qkv = w_in[:, :GDN_QKV].astype(BF16)
    w_z = w_in[:, GDN_QKV:GDN_QKV + GDN_NV].astype(BF16)
    w_ba = w_in[:, GDN_QKV + GDN_NV:].astype(BF16)
    tile = lambda w: pl.BlockSpec((1, ts, w), lambda i, jj: (i, jj, 0))
    q, k, v, z, bcol, gcol, rs, egl = pl.pallas_call(
        _gdn_in_kernel,
        grid=(bn, s // ts),
        in_specs=[tile(d), _const_spec(w_qkv.shape), _const_spec(w_conv.shape), _const_spec(w_z.shape),
                  _const_spec(w_ba.shape), _const_spec((2 * nh, d)), _const_spec((1, nh)), _const_spec((1, nh)),
                  _const_spec((nh, 1)), _const_spec((nh, 1))],
        out_specs=[tile(GDN_NK), tile(GDN_NK), tile(GDN_NV), tile(GDN_NV), tile(nh), tile(nh),
                   pl.BlockSpec((1, 2 * nh, ts), lambda i, jj: (i, 0, jj)),
                   pl.BlockSpec((1, ts // GDN_CHUNK, GDN_NV), lambda i, jj: (i, jj, 0))],
        out_shape=[jax.ShapeDtypeStruct((bn, s, GDN_NK), F32), jax.ShapeDtypeStruct((bn, s, GDN_NK), F32),
                   jax.ShapeDtypeStruct((bn, s, GDN_NV), F32), jax.ShapeDtypeStruct((bn, s, GDN_NV), BF16),
                   jax.ShapeDtypeStruct((bn, s, nh), F32), jax.ShapeDtypeStruct((bn, s, nh), F32),
                   jax.ShapeDtypeStruct((bn, 2 * nh, s), F32),
                   jax.ShapeDtypeStruct((bn, nchunks, GDN_NV), F32)],
        scratch_shapes=[pltpu.VMEM((GDN_HALO + ts, GDN_QKV), F32)],
        compiler_params=_cparams("parallel", "arbitrary"),
        name="gdn_in",
    )(x, w_qkv, w_conv, w_z, w_ba, w_ba.T, a_log.reshape(1, nh), dt_bias.reshape(1, nh),
      a_log.reshape(nh, 1), dt_bias.reshape(nh, 1))

    tc = GDN_TP
    pair = lambda w: pl.BlockSpec((1, tc, w), lambda i, jj, tt: (i, tt, jj))
    whole = lambda w: pl.BlockSpec((1, tc, w), lambda i, jj, tt: (i, tt, 0))
    w, u, qh, kt, intra = pl.pallas_call(
        _gdn_pre_kernel,
        grid=(bn, GDN_K_HEADS, s // tc),
        in_specs=[pair(GDN_DK), pair(GDN_DK), pair(2 * GDN_DV), whole(nh), whole(nh),
                  pl.BlockSpec((1, 2 * nh, tc), lambda i, jj, tt: (i, 0, tt))],
        out_specs=[pair(2 * GDN_DK), pair(2 * GDN_DV), pair(2 * GDN_DK), pair(2 * GDN_DK), pair(2 * GDN_CHUNK)],
        out_shape=[jax.ShapeDtypeStruct((bn, s, 2 * GDN_NK), BF16), jax.ShapeDtypeStruct((bn, s, GDN_NV), F32),
                   jax.ShapeDtypeStruct((bn, s, 2 * GDN_NK), BF16), jax.ShapeDtypeStruct((bn, s, 2 * GDN_NK), BF16),
                   jax.ShapeDtypeStruct((bn, s, GDN_V_HEADS * GDN_CHUNK), BF16)],
        compiler_params=_cparams("parallel", "parallel", "parallel"),
        name="gdn_pre",
    )(q, k, v, bcol, gcol, rs)

    o = pl.pallas_call(
        _gdn_scan_kernel,
        grid=(bn, s // ts),
        in_specs=[tile(2 * GDN_NK), tile(GDN_NV), tile(2 * GDN_NK), tile(2 * GDN_NK),
                  tile(GDN_V_HEADS * GDN_CHUNK),
                  pl.BlockSpec((1, ts // GDN_CHUNK, GDN_NV), lambda i, jj: (i, jj, 0))],
        out_specs=tile(GDN_NV),
        out_shape=jax.ShapeDtypeStruct((bn, s, GDN_NV), F32),
        scratch_shapes=[pltpu.VMEM((GDN_K_HEADS, 2 * GDN_DK, 2 * GDN_DV), F32)],
        compiler_params=_cparams("parallel", "arbitrary"),
        name="gdn_scan",
    )(w, u, qh, kt, intra, egl)

    return pl.pallas_call(
        _gdn_out_kernel,
        grid=(bn, s // ts),
        in_specs=[tile(GDN_NV), tile(GDN_NV), tile(d), _const_spec((1, GDN_DV)), _const_spec(w_out.shape),
                  _const_spec((1, d)), _const_spec((1, d))],
        out_specs=tile(d),
        out_shape=jax.ShapeDtypeStruct(x.shape, F32),
        compiler_params=_cparams("parallel", "parallel"),
        name="gdn_out",
    )(o, z, x, norm_g.reshape(1, GDN_DV), w_out.astype(BF16), g.reshape(1, d), b.reshape(1, d))


ATT_HEADS = D_MODEL // 64
ATT_HEAD_DIM = 64
ROPE_DIM = ATT_HEAD_DIM // 4
ROPE_HALF = ROPE_DIM // 2
ATT_NOPE_DIM = ATT_HEAD_DIM - ROPE_DIM
Q_RANK = D_MODEL // 4
KV_RANK = D_MODEL // 8
IDX_HEADS = 8
IDX_DIM = ATT_HEAD_DIM
TOPK_MAX = 256
QK_WIDTH = 2 * LANES
DSA_TS = 512
DSA_QB = 128
DSA_KT = 512
HEAD_GROUP = 4
MASKED = -1e30
LOG2_E = 1.4426950408889634


def _rope_tab_kernel(pos_ref, inv_ref, rot_ref, sgn_ref, cos_ref, sin_ref):
    ang = pos_ref[0].astype(F32) * inv_ref[...]
    cos_ref[0] = jnp.where(rot_ref[...] > 0.0, jnp.cos(ang), 1.0)
    sin_ref[0] = jnp.sin(ang) * sgn_ref[...]


def _rope_tables(positions):
    bn, s = positions.shape
    lane = jnp.arange(LANES)
    inv = ROPE_THETA ** (-jnp.arange(0, ROPE_DIM, 2, dtype=F32) / ROPE_DIM)
    within = lane % ATT_HEAD_DIM
    rot = (within < ROPE_DIM).astype(F32)
    sgn = jnp.where(within < ROPE_HALF, -1.0, 1.0).astype(F32) * rot
    row = lambda v: v.reshape(1, LANES)
    ts = DSA_TS
    tile = pl.BlockSpec((1, ts, LANES), lambda i, j: (i, j, 0))
    return pl.pallas_call(
        _rope_tab_kernel,
        grid=(bn, s // ts),
        in_specs=[pl.BlockSpec((1, ts, 1), lambda i, j: (i, j, 0))] + [_const_spec((1, LANES))] * 3,
        out_specs=[tile, tile],
        out_shape=[jax.ShapeDtypeStruct((bn, s, LANES), F32)] * 2,
        compiler_params=_cparams("parallel", "parallel"),
        name="rope_tables",
    )(positions.reshape(bn, s, 1), row(inv[lane % ROPE_HALF]), row(rot), row(sgn))


def _rope(v, cosm, sinm):
    lane = lax.broadcasted_iota(jnp.int32, (1, LANES), 1)
    low = (lane % ROPE_DIM) < ROPE_HALF
    cols = []
    for c0 in range(0, v.shape[1], LANES):
        blk = v[:, c0:c0 + LANES]
        partner = jnp.where(low, pltpu.roll(blk, LANES - ROPE_HALF, axis=1), pltpu.roll(blk, ROPE_HALF, axis=1))
        cols.append(blk * cosm + partner * sinm)
    return cols[0] if len(cols) == 1 else jnp.concatenate(cols, axis=1)


def _rms_norm(v, g):
    return v * lax.rsqrt(jnp.mean(v * v, -1, keepdims=True) + RMS_EPS) * g


def _dsa_proj_kernel(x_ref, cos_ref, sin_ref, wdq_ref, qn_ref, wuq_ref, wcomb_ref, wdkv_ref, kvn_ref, wkr_ref,
                     wiq_ref, wik_ref, ikg_ref, ikb_ref, prot_ref, wiwt_ref,
                     q2_ref, k2_ref, qi_ref, ki_ref, wit_ref):
    xb = x_ref[0].astype(BF16)
    cosm, sinm = cos_ref[0], sin_ref[0]
    cq = _rms_norm(_dot(xb, wdq_ref[...]), qn_ref[...]).astype(BF16)
    q = _rope(_dot(cq, wuq_ref[...]), cosm, sinm) * (ATT_HEAD_DIM ** -0.5 * LOG2_E)
    qb16 = q.astype(BF16)
    for g in range(ATT_HEADS // HEAD_GROUP):
        q2_ref[0, :, g * HEAD_GROUP * QK_WIDTH:(g + 1) * HEAD_GROUP * QK_WIDTH] = _dot(
            qb16[:, g * HEAD_GROUP * ATT_HEAD_DIM:(g + 1) * HEAD_GROUP * ATT_HEAD_DIM], wcomb_ref[g]).astype(BF16)
    ckv = _rms_norm(_dot(xb, wdkv_ref[...]), kvn_ref[...])
    kr = _rope(_dot(xb, wkr_ref[...]), cosm, sinm)
    k2_ref[0] = jnp.concatenate([ckv, kr], axis=1).astype(BF16)
    qi = _rope(_dot(cq, wiq_ref[...]), cosm, sinm).astype(BF16)
    for h in range(IDX_HEADS):
        qi_ref[0, h] = qi[:, h * IDX_DIM:(h + 1) * IDX_DIM]
    kin = _layer_norm(_dot(xb, wik_ref[...]), ikg_ref[...], ikb_ref[...])
    ki = kin * cosm[:, :IDX_DIM] + _hdot(kin, prot_ref[...]) * sinm[:, :IDX_DIM]
    ki_ref[0] = ki.astype(BF16)
    wit = lax.dot_general(wiwt_ref[...], xb, NT_DIMS, preferred_element_type=F32)
    wit_ref[0] = wit * (IDX_HEADS * IDX_DIM) ** -0.5


def _dsa_attn_kernel(q2_ref, k2_ref, qi_ref, ki_ref, wit_ref, o_ref,
                     key_ref, mask_ref, q2s_ref, lga_ref, lgb_ref, pa_ref, pb_ref, acc_ref):
    qb, kt_sz = DSA_QB, DSA_KT
    i = pl.program_id(1)
    n_tiles = (i * qb) // kt_sz + 1
    q_pos = i * qb + lax.broadcasted_iota(jnp.int32, (1, qb), 1)

    def score_tile(t, carry):
        r0 = pl.multiple_of(t * kt_sz, kt_sz)
        kblk = ki_ref[0, pl.ds(r0, kt_sz), :]
        acc = jnp.zeros((kt_sz, qb), F32)
        for h in range(IDX_HEADS):
            sc = lax.dot_general(kblk, qi_ref[0, h], NT_DIMS, preferred_element_type=F32)
            acc = acc + jnp.maximum(sc, 0.0) * wit_ref[0, h:h + 1, :]
        k_pos = r0 + lax.broadcasted_iota(jnp.int32, (kt_sz, 1), 0)
        acc = jnp.where(k_pos <= q_pos, acc, -jnp.inf)
        bits = pltpu.bitcast(acc, jnp.int32)
        key_ref[pl.ds(r0, kt_sz), :] = bits ^ ((bits >> 31) & jnp.int32(0x7FFFFFFF))
        return carry

    lax.fori_loop(0, n_tiles, score_tile, 0)

    def count(pred):
        def body(t, acc):
            r0 = pl.multiple_of(t * kt_sz, kt_sz)
            k_pos = r0 + lax.broadcasted_iota(jnp.int32, (kt_sz, 1), 0)
            hit = jnp.where(pred(key_ref[pl.ds(r0, kt_sz), :], k_pos), 1, 0).astype(jnp.int32)
            return acc + jnp.sum(hit.reshape(kt_sz // SUBLANES, SUBLANES, qb), axis=0)
        acc = lax.fori_loop(0, n_tiles, body, jnp.zeros((SUBLANES, qb), jnp.int32))
        return jnp.sum(acc, axis=0, keepdims=True)

    int_min = jnp.int32(-2 ** 31)
    topk = jnp.int32(TOPK_MAX)

    def search_bit(b, thr):
        cand = thr + lax.shift_left(jnp.int32(1), jnp.int32(31) - b)
        return jnp.where(count(lambda kb, kp: kb >= cand) >= topk, cand, thr)

    few_keys = (i + 1) * qb <= TOPK_MAX
    thr0 = jnp.full((1, qb), int_min, jnp.int32)
    thr = lax.cond(few_keys, lambda: thr0, lambda: lax.fori_loop(0, 32, search_bit, thr0))

    need = topk - count(lambda kb, kp: kb > thr)
    n_eq = count(lambda kb, kp: kb == thr)
    full_cut = jnp.full((1, qb), jnp.int32(2 ** 30), jnp.int32)

    def search_cut():
        def cut_bit(b, cut):
            cand = cut + lax.shift_left(jnp.int32(1), jnp.int32(12) - b)
            below = count(lambda kb, kp: (kb == thr) & (kp < cand))
            return jnp.where(below < need, cand, cut)
        return lax.fori_loop(0, 13, cut_bit, jnp.zeros((1, qb), jnp.int32))

    tied = jnp.logical_and(jnp.logical_not(few_keys), jnp.max(jnp.where(n_eq != need, 1, 0)) > 0)
    cut = lax.cond(tied, search_cut, lambda: full_cut)

    def mask_tile(t, carry):
        r0 = pl.multiple_of(t * kt_sz, kt_sz)
        kb = key_ref[pl.ds(r0, kt_sz), :]
        k_pos = r0 + lax.broadcasted_iota(jnp.int32, (kt_sz, 1), 0)
        keep = ((kb > thr) | ((kb == thr) & (k_pos <= cut))) & (k_pos <= q_pos)
        mask_ref[:, pl.ds(r0, kt_sz)] = jnp.where(keep, 0.0, MASKED).astype(F32).T
        return carry

    lax.fori_loop(0, n_tiles, mask_tile, 0)

    nh = ATT_HEADS
    half = kt_sz // 2
    for h in range(nh):
        q2s_ref[h * qb:(h + 1) * qb, :] = q2_ref[0, :, h * QK_WIDTH:(h + 1) * QK_WIDTH]
    acc_ref[...] = jnp.zeros(acc_ref.shape, F32)
    vlane = lax.broadcasted_iota(jnp.int32, (1, QK_WIDTH), 1)
    ones_col = jnp.where(vlane == KV_RANK, 1.0, 0.0).astype(BF16)

    def logits_into(dst_ref, key0):
        dst_ref[...] = lax.dot_general(q2s_ref[...], k2_ref[0, pl.ds(pl.multiple_of(key0, half), half), :],
                                       NT_DIMS, preferred_element_type=F32)

    def softmax_pv(src_ref, prob_ref, key0, m):
        key0 = pl.multiple_of(key0, half)
        mask = mask_ref[:, pl.ds(key0, half)]
        wide = lambda v: jnp.concatenate([v] * (half // LANES), axis=1)
        m_parts, alpha_parts = [], []
        for h in range(nh):
            rows = slice(h * qb, (h + 1) * qb)
            lg = src_ref[rows, :] + mask
            m_new = jnp.maximum(m[rows], jnp.max(lg, -1, keepdims=True))
            prob_ref[rows, :] = jnp.exp2(lg - wide(m_new)).astype(BF16)
            alpha_parts.append(jnp.exp2(m[rows] - m_new))
            m_parts.append(m_new)
        v1 = jnp.where(vlane < KV_RANK, k2_ref[0, pl.ds(key0, half), :], ones_col)
        alpha = jnp.concatenate(alpha_parts, axis=0)
        acc_ref[...] = acc_ref[...] * jnp.concatenate([alpha] * (QK_WIDTH // LANES), axis=1) + _dot(prob_ref[...], v1)
        return jnp.concatenate(m_parts, axis=0)

    logits_into(lga_ref, 0)
    last_key0 = (n_tiles - 1) * kt_sz

    def attn_tile(t, m):
        key0 = t * kt_sz
        logits_into(lgb_ref, key0 + half)
        m = softmax_pv(lga_ref, pa_ref, key0, m)
        logits_into(lga_ref, jnp.minimum(key0 + kt_sz, last_key0))
        return softmax_pv(lgb_ref, pb_ref, key0 + half, m)

    lax.fori_loop(0, n_tiles, attn_tile, jnp.full((nh * qb, LANES), MASKED, F32))
    for h in range(nh):
        acc = acc_ref[h * qb:(h + 1) * qb, :]
        o_ref[0, :, h * KV_RANK:(h + 1) * KV_RANK] = (
            acc[:, :KV_RANK] / acc[:, KV_RANK:KV_RANK + 1]).astype(o_ref.dtype)


def _dsa_out_kernel(o_ref, x_ref, wuv_ref, wo_ref, g_ref, b_ref, y_ref):
    o = jnp.concatenate(
        [_dot(o_ref[0, :, g * HEAD_GROUP * KV_RANK:(g + 1) * HEAD_GROUP * KV_RANK], wuv_ref[g])
         for g in range(ATT_HEADS // HEAD_GROUP)], axis=1).astype(BF16)
    mix = _dot(o, wo_ref[...])
    y_ref[0] = _layer_norm(DN_ALPHA * x_ref[0] + mix, g_ref[...], b_ref[...])


def _block_diag(blocks):
    n, r, c = blocks.shape
    idx = jnp.arange(n)
    return jnp.zeros((n, r, n, c), blocks.dtype).at[idx, :, idx, :].set(blocks).reshape(n * r, n * c)


def _dsa_ln(x, positions, w_dq, q_norm, w_uq, w_dkv, kv_norm, w_kr, w_uk, w_uv, w_o,
            w_iq, w_ik, ik_g, ik_b, w_iw, g, b):
    bn, s, d = x.shape
    ts, nh = DSA_TS, ATT_HEADS
    cosm, sinm = _rope_tables(positions)

    head_map = jnp.zeros((nh, ATT_HEAD_DIM, QK_WIDTH), F32)
    head_map = head_map.at[:, ROPE_DIM:, :KV_RANK].set(w_uk)
    head_map = head_map.at[:, :ROPE_DIM, KV_RANK:KV_RANK + ROPE_DIM].set(jnp.eye(ROPE_DIM, dtype=F32))
    groups = range(0, nh, HEAD_GROUP)
    w_comb = jnp.stack([_block_diag(head_map[h:h + HEAD_GROUP]) for h in groups]).astype(BF16)
    w_kr_pad = jnp.zeros((d, LANES), F32).at[:, :ROPE_DIM].set(w_kr).astype(BF16)
    lane = jnp.arange(IDX_DIM)
    partner = jnp.where(lane < ROPE_HALF, lane + ROPE_HALF, lane - ROPE_HALF)
    p_rot = ((lane[:, None] == partner[None, :]) & (lane[None, :] < ROPE_DIM)).astype(F32)

    tile = lambda w: pl.BlockSpec((1, ts, w), lambda i, j: (i, j, 0))
    consts = [w_dq.astype(BF16), q_norm.reshape(1, -1), w_uq.astype(BF16), w_comb, w_dkv.astype(BF16),
              kv_norm.reshape(1, -1), w_kr_pad, w_iq.astype(BF16), w_ik.astype(BF16), ik_g.reshape(1, -1),
              ik_b.reshape(1, -1), p_rot, w_iw.T.astype(BF16)]
    q2, k2, qi, ki, wit = pl.pallas_call(
        _dsa_proj_kernel,
        grid=(bn, s // ts),
        in_specs=[tile(d), tile(LANES), tile(LANES)] + [_const_spec(c.shape) for c in consts],
        out_specs=[tile(nh * QK_WIDTH), tile(QK_WIDTH),
                   pl.BlockSpec((1, IDX_HEADS, ts, IDX_DIM), lambda i, j: (i, 0, j, 0)),
                   tile(IDX_DIM), pl.BlockSpec((1, IDX_HEADS, ts), lambda i, j: (i, 0, j))],
        out_shape=[jax.ShapeDtypeStruct((bn, s, nh * QK_WIDTH), BF16), jax.ShapeDtypeStruct((bn, s, QK_WIDTH), BF16),
                   jax.ShapeDtypeStruct((bn, IDX_HEADS, s, IDX_DIM), BF16),
                   jax.ShapeDtypeStruct((bn, s, IDX_DIM), BF16), jax.ShapeDtypeStruct((bn, IDX_HEADS, s), F32)],
        compiler_params=_cparams("parallel", "parallel"),
        name="dsa_proj",
    )(x, cosm, sinm, *consts)

    qb = DSA_QB
    o_lat = pl.pallas_call(
        _dsa_attn_kernel,
        grid=(bn, s // qb),
        in_specs=[pl.BlockSpec((1, qb, nh * QK_WIDTH), lambda i, j: (i, j, 0)),
                  pl.BlockSpec((1, s, QK_WIDTH), lambda i, j: (i, 0, 0)),
                  pl.BlockSpec((1, IDX_HEADS, qb, IDX_DIM), lambda i, j: (i, 0, j, 0)),
                  pl.BlockSpec((1, s, IDX_DIM), lambda i, j: (i, 0, 0)),
                  pl.BlockSpec((1, IDX_HEADS, qb), lambda i, j: (i, 0, j))],
        out_specs=pl.BlockSpec((1, qb, nh * KV_RANK), lambda i, j: (i, j, 0)),
        out_shape=jax.ShapeDtypeStruct((bn, s, nh * KV_RANK), BF16),
        scratch_shapes=[pltpu.VMEM((s, qb), jnp.int32), pltpu.VMEM((qb, s), F32),
                        pltpu.VMEM((nh * qb, QK_WIDTH), BF16),
                        pltpu.VMEM((nh * qb, DSA_KT // 2), F32), pltpu.VMEM((nh * qb, DSA_KT // 2), F32),
                        pltpu.VMEM((nh * qb, DSA_KT // 2), BF16), pltpu.VMEM((nh * qb, DSA_KT // 2), BF16),
                        pltpu.VMEM((nh * qb, QK_WIDTH), F32)],
        compiler_params=_cparams("parallel", "parallel"),
        name="dsa_attn",
    )(q2, k2, qi, ki, wit)

    w_uv_bd = jnp.stack([_block_diag(w_uv[h:h + HEAD_GROUP]) for h in groups]).astype(BF16)
    return pl.pallas_call(
        _dsa_out_kernel,
        grid=(bn, s // ts),
        in_specs=[tile(nh * KV_RANK), tile(d), _const_spec(w_uv_bd.shape), _const_spec(w_o.shape),
                  _const_spec((1, d)), _const_spec((1, d))],
        out_specs=tile(d),
        out_shape=jax.ShapeDtypeStruct(x.shape, F32),
        compiler_params=_cparams("parallel", "parallel"),
        name="dsa_out",
    )(o_lat, x, w_uv_bd, w_o.astype(BF16), g.reshape(1, d), b.reshape(1, d))


def kernel(x, positions, ln_g, ln_b, ffn_w_gate, ffn_w_up, ffn_w_down,
           conv_w_in, conv_w_dw, conv_ln_g, conv_ln_b, conv_w_out,
           sc_w_in, sc_w_conv, sc_w_out,
           dsa_w_dq, dsa_q_norm, dsa_w_uq, dsa_w_dkv, dsa_kv_norm, dsa_w_kr,
           dsa_w_uk, dsa_w_uv, dsa_w_o, dsa_w_iq, dsa_w_ik, dsa_ik_ln_g, dsa_ik_ln_b, dsa_w_iw,
           gdn_w_in, gdn_w_conv, gdn_a_log, gdn_dt_bias, gdn_norm_g, gdn_w_out):
    bn, s, d = x.shape

    def ffn(h, i, half):
        return _ffn_ln(h.reshape(bn * s, d), ffn_w_gate[i, half], ffn_w_up[i, half], ffn_w_down[i, half],
                       ln_g[i, 2 * half], ln_b[i, 2 * half]).reshape(bn, s, d)

    for i in range(DEPTH):
        m, j = i % N_MIXERS, i // N_MIXERS
        x = ffn(x, i, 0)
        g, b = ln_g[i, 1], ln_b[i, 1]
        if m == 0:
            x = _conformer_ln(x, conv_w_in[j], conv_w_dw[j], conv_ln_g[j], conv_ln_b[j], conv_w_out[j], g, b)
        elif m == 1:
            x = _sconv_ln(x, sc_w_in[j], sc_w_conv[j], sc_w_out[j], g, b)
        elif m == 2:
            x = _dsa_ln(x, positions, dsa_w_dq[j], dsa_q_norm[j], dsa_w_uq[j], dsa_w_dkv[j], dsa_kv_norm[j],
                        dsa_w_kr[j], dsa_w_uk[j], dsa_w_uv[j], dsa_w_o[j], dsa_w_iq[j], dsa_w_ik[j],
                        dsa_ik_ln_g[j], dsa_ik_ln_b[j], dsa_w_iw[j], g, b)
        else:
            x = _gdn_ln(x, gdn_w_in[j], gdn_w_conv[j], gdn_a_log[j], gdn_dt_bias[j], gdn_norm_g[j],
                        gdn_w_out[j], g, b)
        x = ffn(x, i, 1)
    return x
```
